```python
import jax, jax.numpy as jnp
from jax import lax
import numpy as np

D_MODEL = 1024
BATCH = 32
SEQ = 2048
DEPTH = 1

D_MIX = D_MODEL
D_CONV = D_MIX // 2
D_ATT = D_MIX - D_CONV
N_ATT_HEADS = 8
HEAD_DIM = D_ATT // N_ATT_HEADS
CONV_WIDTH = 31
CONV_PAD = CONV_WIDTH // 2
DILATED_PATTERNS = ((128, 1), (512, 4), (2048, 16))
D_IN = 2 * D_CONV + 3 * D_ATT
D_FF = -(-8 * D_MODEL // (3 * 256)) * 256
N_MOD = 6
EPS = 1e-6
NEG_INF = -1e30

kernel_name = "hybrid_conformer_dilated_attn_block"


def _rms_norm(x, g):
    xf = x.astype(jnp.float32)
    y = xf * lax.rsqrt(jnp.mean(xf * xf, axis=-1, keepdims=True) + EPS)
    return (y * g.astype(jnp.float32)).astype(x.dtype)


def _layer_norm(x, g, b):
    xf = x.astype(jnp.float32)
    mu = jnp.mean(xf, axis=-1, keepdims=True)
    var = jnp.mean(jnp.square(xf - mu), axis=-1, keepdims=True)
    y = (xf - mu) * lax.rsqrt(var + EPS)
    return (y * g.astype(jnp.float32) + b.astype(jnp.float32)).astype(x.dtype)


def _modulate(h, shift, scale):
    return h * (1 + scale) + shift


def _alibi_slopes(n_heads):
    return jnp.asarray(2.0 ** (-8.0 * np.arange(1, n_heads + 1) / n_heads), dtype=jnp.float32)


def _dilated_band_attention(q, k, v, slopes, window, dilation):
    B, S, H, E = q.shape
    radius = window // (2 * dilation)
    blk = radius
    n_units = -(-S // dilation)
    nb = -(-n_units // blk)
    s_pad = nb * blk * dilation
    pad = ((0, 0), (0, s_pad - S), (0, 0), (0, 0))

    def to_blocks(t):
        return jnp.pad(t, pad).reshape(B, nb, blk, dilation, H, E)

    def band(t):
        tp = jnp.pad(t, ((0, 0), (1, 1), (0, 0), (0, 0), (0, 0), (0, 0)))
        return jnp.concatenate([tp[:, :-2], tp[:, 1:-1], tp[:, 2:]], axis=2)

    qb = to_blocks(q)
    kw = band(to_blocks(k))
    vw = band(to_blocks(v))
    s = jnp.einsum('bnqrhe,bnkrhe->bnrhqk', qb, kw)

    rel = jnp.arange(3 * blk)[None, :] - blk - jnp.arange(blk)[:, None]
    n_idx = jnp.arange(nb)[:, None, None]
    r_idx = jnp.arange(dilation)[None, :, None]
    key_pos = ((n_idx - 1) * blk + jnp.arange(3 * blk)[None, None, :]) * dilation + r_idx
    valid = (jnp.abs(rel) <= radius)[None, None] & ((key_pos >= 0) & (key_pos < S))[:, :, None, :]
    bias = -slopes[:, None, None] * (dilation * jnp.abs(rel)).astype(jnp.float32)[None]
    s = jnp.where(valid[None, :, :, None], s + bias[None, None, None], NEG_INF)

    lse = jax.nn.logsumexp(s, axis=-1)
    p = jnp.exp(s - lse[..., None])
    o = jnp.einsum('bnrhqk,bnkrhe->bnqrhe', p, vw).reshape(B, s_pad, H, E)[:, :S]
    lse = lse.transpose(0, 1, 4, 2, 3).reshape(B, s_pad, H)[:, :S]
    return o, lse


def _conv_module(a, g, w_dw, b_dw, g_ln, b_ln):
    u = a * jax.nn.sigmoid(g)
    u = lax.conv_general_dilated(u, w_dw[:, None, :].astype(u.dtype), window_strides=(1,),
                                 padding=[(CONV_PAD, CONV_PAD)],
                                 dimension_numbers=('NWC', 'WIO', 'NWC'),
                                 feature_group_count=D_CONV) + b_dw
    return jax.nn.silu(_layer_norm(u, g_ln, b_ln))


def _dilated_attention(q, k, v, g_q, g_k):
    B, S, _ = q.shape
    q = q.reshape(B, S, N_ATT_HEADS, HEAD_DIM)
    k = k.reshape(B, S, N_ATT_HEADS, HEAD_DIM)
    v = v.reshape(B, S, N_ATT_HEADS, HEAD_DIM).astype(jnp.float32)
    q = _rms_norm(q, g_q).astype(jnp.float32) * (HEAD_DIM ** -0.5)
    k = _rms_norm(k, g_k).astype(jnp.float32)
    slopes = _alibi_slopes(N_ATT_HEADS)
    results = [_dilated_band_attention(q, k, v, slopes, w, d) for (w, d) in DILATED_PATTERNS]
    lses = jnp.stack([r[1] for r in results], axis=0)
    wts = jax.nn.softmax(lses, axis=0)
    o = sum(wts[i][..., None] * results[i][0] for i in range(len(results)))
    return o.reshape(B, S, D_ATT)


def setup_inputs(seed: int = 0) -> dict:
    key = jax.random.key(seed)
    ks = jax.random.split(key, 18)
    f32 = jnp.float32
    nrm = lambda k, shape, s: jax.random.normal(k, shape, f32) * s
    gain = lambda k, shape: 1.0 + 0.02 * jax.random.normal(k, shape, f32)
    return {
        "x": jax.random.normal(ks[0], (BATCH, SEQ, D_MODEL), f32),
        "c": jax.random.normal(ks[1], (BATCH, D_MODEL), f32),
        "w_ada": nrm(ks[2], (DEPTH, D_MODEL, N_MOD * D_MODEL), D_MODEL ** -0.5),
        "b_ada": nrm(ks[3], (DEPTH, N_MOD * D_MODEL), 0.02),
        "g_mix": gain(ks[4], (DEPTH, D_MODEL)),
        "w_in": nrm(ks[5], (DEPTH, D_MODEL, D_IN), D_MODEL ** -0.5),
        "w_dw": nrm(ks[6], (DEPTH, CONV_WIDTH, D_CONV), CONV_WIDTH ** -0.5),
        "b_dw": nrm(ks[7], (DEPTH, D_CONV), 0.02),
        "g_conv_ln": gain(ks[8], (DEPTH, D_CONV)),
        "b_conv_ln": nrm(ks[9], (DEPTH, D_CONV), 0.02),
        "g_q": gain(ks[10], (DEPTH, HEAD_DIM)),
        "g_k": gain(ks[11], (DEPTH, HEAD_DIM)),
        "w_out": nrm(ks[12], (DEPTH, D_MIX, D_MODEL), D_MIX ** -0.5),
        "g_ffn": gain(ks[13], (DEPTH, D_MODEL)),
        "w_gate": nrm(ks[14], (DEPTH, D_MODEL, D_FF), D_MODEL ** -0.5),
        "w_up": nrm(ks[15], (DEPTH, D_MODEL, D_FF), D_MODEL ** -0.5),
        "w_down": nrm(ks[16], (DEPTH, D_FF, D_MODEL), D_FF ** -0.5),
    }


def reference(x, c, w_ada, b_ada, g_mix, w_in, w_dw, b_dw, g_conv_ln, b_conv_ln,
              g_q, g_k, w_out, g_ffn, w_gate, w_up, w_down):
    B, S, D = x.shape
    split_at = [D_CONV, 2 * D_CONV, 2 * D_CONV + D_ATT, 2 * D_CONV + 2 * D_ATT]
    for l in range(DEPTH):
        mod = jax.nn.silu(c) @ w_ada[l] + b_ada[l]
        shift_m, scale_m, gate_m, shift_f, scale_f, gate_f = [
            m[:, None, :] for m in jnp.split(mod, N_MOD, axis=-1)]

        h = _modulate(_rms_norm(x, g_mix[l]), shift_m, scale_m)
        proj = h @ w_in[l]
        a, g, q, k, v = jnp.split(proj, split_at, axis=-1)
        y_conv = _conv_module(a, g, w_dw[l], b_dw[l], g_conv_ln[l], b_conv_ln[l])
        y_att = _dilated_attention(q, k, v, g_q[l], g_k[l]).astype(x.dtype)
        mix = jnp.concatenate([y_conv, y_att], axis=-1) @ w_out[l]
        x = x + gate_m * mix

        h = _modulate(_rms_norm(x, g_ffn[l]), shift_f, scale_f)
        f = (jax.nn.silu(h @ w_gate[l]) * (h @ w_up[l])) @ w_down[l]
        x = x + gate_f * f
    return x
```

```python
import functools

import jax
import jax.numpy as jnp
import numpy as np
from jax import lax
from jax.experimental import pallas as pl
from jax.experimental.pallas import tpu as pltpu

F32 = jnp.float32
BF16 = jnp.bfloat16

D_MODEL = 1024
D_CONV = 512
D_ATT = 512
N_ATT_HEADS = 8
HEAD_DIM = 64
CONV_WIDTH = 31
CONV_PAD = CONV_WIDTH // 2
DILATIONS = (1, 4, 16)
RADIUS = 64
D_IN = 2 * D_CONV + 3 * D_ATT
D_FF = 2816
N_MOD = 6
EPS = 1e-6
NEG_INF = -1e30

LANES = 128
Q_BLK = 128
K_WIN = Q_BLK + 2 * RADIUS
FF_CHUNK = 256
MIB = 1024 * 1024


def _dot(a, b):
    return jnp.dot(a, b, preferred_element_type=F32)


def _split_bf16(a):
    hi = a.astype(BF16)
    lo = (a - hi.astype(F32)).astype(BF16)
    return hi, lo


def _ada_kernel(c_ref, w_ref, b_ref, o_ref):
    c = c_ref[...]
    a_hi, a_lo = _split_bf16(c * jax.nn.sigmoid(c))
    w_hi, w_lo = _split_bf16(w_ref[...])
    o_ref[...] = _dot(a_hi, w_hi) + _dot(a_hi, w_lo) + _dot(a_lo, w_hi) + b_ref[...]


def _ada(c, w, b):
    bsz, d = c.shape
    n = w.shape[1]
    tn = 1024
    return pl.pallas_call(
        _ada_kernel,
        grid=(n // tn,),
        in_specs=[pl.BlockSpec((bsz, d), lambda j: (0, 0)),
                  pl.BlockSpec((d, tn), lambda j: (0, j)),
                  pl.BlockSpec((1, tn), lambda j: (0, j))],
        out_specs=pl.BlockSpec((bsz, tn), lambda j: (0, j)),
        out_shape=jax.ShapeDtypeStruct((bsz, n), F32),
        compiler_params=pltpu.CompilerParams(dimension_semantics=("arbitrary",),
                                             vmem_limit_bytes=32 * MIB),
        name="ada",
    )(c, w, b.reshape(1, n))


def _head_norm(t, ones, gain):
    ssq = _dot((t * t).astype(BF16), ones)
    return t * lax.rsqrt(ssq * (1.0 / HEAD_DIM) + EPS) * gain


def _inp_kernel(x_ref, mod_ref, g_ref, w_ref, ones_ref, gq_ref, gk_ref,
                u_ref, q_ref, k_ref, v_ref):
    x = x_ref[0]
    shift = mod_ref[0, 0:1, :]
    scale = mod_ref[0, 1:2, :]
    r = lax.rsqrt(jnp.mean(x * x, axis=-1, keepdims=True) + EPS)
    h = ((x * r) * g_ref[...]) * (1.0 + scale) + shift
    hb = h.astype(BF16)
    a = _dot(hb, w_ref[:, 0:D_CONV])
    g = _dot(hb, w_ref[:, D_CONV:2 * D_CONV])
    u_ref[0] = (a * jax.nn.sigmoid(g)).astype(BF16)
    ones = ones_ref[...]
    o = 2 * D_CONV
    q = _dot(hb, w_ref[:, o:o + D_ATT])
    q_ref[0] = (_head_norm(q, ones, gq_ref[...]) * (HEAD_DIM ** -0.5)).astype(BF16)
    k = _dot(hb, w_ref[:, o + D_ATT:o + 2 * D_ATT])
    k_ref[0] = _head_norm(k, ones, gk_ref[...]).astype(BF16)
    v_ref[0] = _dot(hb, w_ref[:, o + 2 * D_ATT:o + 3 * D_ATT]).astype(BF16)


def _inp(x, mod, g_mix, w_in, g_q, g_k, ts):
    bsz, s, d = x.shape
    head = np.arange(D_ATT) // HEAD_DIM
    ones = jnp.asarray(head[:, None] == head[None, :], dtype=BF16)
    gq = jnp.tile(g_q, N_ATT_HEADS).reshape(1, D_ATT)
    gk = jnp.tile(g_k, N_ATT_HEADS).reshape(1, D_ATT)
    const = dict(pipeline_mode=pl.Buffered(1))
    row = lambda b, i: (b, i, 0)
    fix = lambda b, i: (0, 0)
    out = jax.ShapeDtypeStruct((bsz, s, D_ATT), BF16)
    return pl.pallas_call(
        _inp_kernel,
        grid=(bsz, s // ts),
        in_specs=[pl.BlockSpec((1, ts, d), row),
                  pl.BlockSpec((1, N_MOD, d), lambda b, i: (b, 0, 0)),
                  pl.BlockSpec((1, d), fix, **const),
                  pl.BlockSpec((d, D_IN), fix, **const),
                  pl.BlockSpec((D_ATT, D_ATT), fix, **const),
                  pl.BlockSpec((1, D_ATT), fix, **const),
                  pl.BlockSpec((1, D_ATT), fix, **const)],
        out_specs=[pl.BlockSpec((1, ts, D_ATT), row)] * 4,
        out_shape=[out] * 4,
        compiler_params=pltpu.CompilerParams(dimension_semantics=("arbitrary", "arbitrary"),
                                             vmem_limit_bytes=48 * MIB),
        name="inp",
    )(x, mod, g_mix.reshape(1, d), w_in, ones, gq, gk)


def _conv_kernel(u_ref, w_ref, b_ref, g_ref, beta_ref, y_ref, upad_ref, *, rows):
    s = u_ref.shape[1]
    halo = 16
    n_cb = D_CONV // LANES
    zeros = jnp.zeros((halo, LANES), F32)
    for c in range(n_cb):
        upad_ref[c, 0:halo, :] = zeros
        upad_ref[c, halo + s:halo + s + halo, :] = zeros

    def fill(i, carry):
        t0 = pl.multiple_of(i * rows, rows)
        uf = u_ref[0, pl.ds(t0, rows), :].astype(F32)
        for c in range(n_cb):
            upad_ref[c, pl.ds(halo + t0, rows), :] = uf[:, c * LANES:(c + 1) * LANES]
        return carry

    lax.fori_loop(0, s // rows, fill, 0)

    def block(i, carry):
        t0 = pl.multiple_of(i * rows, rows)
        conv = []
        for c in range(n_cb):
            cs = slice(c * LANES, (c + 1) * LANES)
            acc = jnp.zeros((rows, LANES), F32)
            for k in range(CONV_WIDTH):
                tap = upad_ref[c, pl.ds(t0 + (halo - CONV_PAD) + k, rows), :]
                acc = acc + w_ref[k:k + 1, cs] * tap
            conv.append(acc + b_ref[:, cs])
        tot = conv[0] + conv[1] + conv[2] + conv[3]
        mu = jnp.sum(tot, axis=-1, keepdims=True) * (1.0 / D_CONV)
        cen = [t - mu for t in conv]
        sq = cen[0] * cen[0] + cen[1] * cen[1] + cen[2] * cen[2] + cen[3] * cen[3]
        var = jnp.sum(sq, axis=-1, keepdims=True) * (1.0 / D_CONV)
        rstd = lax.rsqrt(var + EPS)
        for c in range(D_CONV // LANES):
            cs = slice(c * LANES, (c + 1) * LANES)
            z = (cen[c] * rstd) * g_ref[:, cs] + beta_ref[:, cs]
            y_ref[0, pl.ds(t0, rows), cs] = (z * jax.nn.sigmoid(z)).astype(BF16)
        return carry

    lax.fori_loop(0, s // rows, block, 0)


def _conv(u, w_dw, b_dw, g_ln, b_ln):
    bsz, s, _ = u.shape
    rows = 32
    fix = lambda b: (0, 0)
    vec = pl.BlockSpec((1, D_CONV), fix)
    return pl.pallas_call(
        functools.partial(_conv_kernel, rows=rows),
        grid=(bsz,),
        in_specs=[pl.BlockSpec((1, s, D_CONV), lambda b: (b, 0, 0)),
                  pl.BlockSpec((CONV_WIDTH, D_CONV), fix), vec, vec, vec],
        out_specs=pl.BlockSpec((1, s, D_CONV), lambda b: (b, 0, 0)),
        out_shape=jax.ShapeDtypeStruct((bsz, s, D_CONV), BF16),
        scratch_shapes=[pltpu.VMEM((D_CONV // LANES, s + 32, LANES), F32)],
        compiler_params=pltpu.CompilerParams(dimension_semantics=("arbitrary",),
                                             vmem_limit_bytes=32 * MIB),
        name="conv",
    )(u, w_dw, b_dw.reshape(1, D_CONV), g_ln.reshape(1, D_CONV), b_ln.reshape(1, D_CONV))


def _att_kernel(slopes_ref, q_ref, k_ref, v_ref, o_ref,
                bias_ref, bias3_ref, tmp_ref, qm_ref, kp_ref, vp_ref,
                acc1_ref, acc2_ref, acc3_ref, m1_ref, m2_ref, m3_ref, l1_ref, l2_ref, l3_ref):
    hp = pl.program_id(0)
    s = q_ref.shape[1]
    n_blk = s // Q_BLK
    lane = lax.broadcasted_iota(jnp.int32, (1, LANES), 1)
    even = lane < HEAD_DIM

    @pl.when(pl.program_id(1) == 0)
    def _build_bias():
        def table(width, off, dil):
            row = lax.broadcasted_iota(jnp.int32, (2 * Q_BLK, width), 0)
            col = lax.broadcasted_iota(jnp.int32, (2 * Q_BLK, width), 1)
            slope = jnp.where(row < Q_BLK, slopes_ref[2 * hp], slopes_ref[2 * hp + 1])
            dist = jnp.abs(col - (row & (Q_BLK - 1)) - off)
            return jnp.where(dist <= RADIUS, -slope * (dil * dist).astype(F32), NEG_INF)

        for var, off in enumerate((0, RADIUS, 2 * RADIUS)):
            for p, dil in enumerate(DILATIONS[:2]):
                bias_ref[p, var] = table(K_WIN, off, dil)
        bias3_ref[...] = table(Q_BLK, 0, DILATIONS[2])

    def put_q(p, blk0, rows_bf16):
        zero = jnp.zeros_like(rows_bf16)
        for j in range(rows_bf16.shape[0] // Q_BLK):
            blk = rows_bf16[j * Q_BLK:(j + 1) * Q_BLK]
            z = zero[:Q_BLK]
            qm_ref[p, blk0 + j, 0:Q_BLK, :] = jnp.where(even, blk, z)
            qm_ref[p, blk0 + j, Q_BLK:2 * Q_BLK, :] = jnp.where(even, z, blk)

    put_q(0, 0, q_ref[0])
    tmp_ref[...] = q_ref[0].astype(F32)
    for p, dil in ((1, DILATIONS[1]), (2, DILATIONS[2])):
        n = s // dil
        for r in range(dil):
            put_q(p, r * (n // Q_BLK), tmp_ref[pl.ds(r, n, stride=dil), :].astype(BF16))
    for src_ref, dst_ref in ((k_ref, kp_ref), (v_ref, vp_ref)):
        tmp_ref[...] = src_ref[0].astype(F32)
        for p, dil in ((0, DILATIONS[1]), (1, DILATIONS[2])):
            n = s // dil
            for r in range(dil):
                dst_ref[p, r * n:(r + 1) * n, :] = tmp_ref[pl.ds(r, n, stride=dil), :].astype(BF16)

    def attend(lhs, kw, vw, bias):
        sc = lax.dot_general(lhs, kw, (((1,), (1,)), ((), ())), preferred_element_type=F32) + bias
        m = jnp.max(sc, axis=-1, keepdims=True)
        e = jnp.exp(sc - m)
        l = jnp.sum(e, axis=-1, keepdims=True)
        pv = _dot(e.astype(BF16), vw)
        o = jnp.where(even, pv[:Q_BLK], pv[Q_BLK:])
        mb = jnp.where(even, m[:Q_BLK], m[Q_BLK:])
        lb = jnp.where(even, l[:Q_BLK], l[Q_BLK:])
        return o, mb, lb

    def window(i, n_sub):
        ws = jnp.clip(i * Q_BLK - RADIUS, 0, (n_sub - 2) * Q_BLK)
        var = jnp.where(i == 0, 0, jnp.where(i == n_sub - 1, 2, 1))
        return pl.multiple_of(ws, RADIUS), var

    def p1_body(i, carry):
        ws, var = window(i, n_blk)
        o, mb, lb = attend(qm_ref[0, i], k_ref[0, pl.ds(ws, K_WIN), :],
                           v_ref[0, pl.ds(ws, K_WIN), :], bias_ref[0, var])
        t0 = pl.multiple_of(i * Q_BLK, Q_BLK)
        acc1_ref[pl.ds(t0, Q_BLK), :] = o
        m1_ref[pl.ds(t0, Q_BLK), :] = mb
        l1_ref[pl.ds(t0, Q_BLK), :] = lb
        return carry

    lax.fori_loop(0, n_blk, p1_body, 0)

    d2 = DILATIONS[1]
    sub2 = n_blk // d2

    def p2_body(blk, carry):
        r = blk // sub2
        i = blk % sub2
        ws, var = window(i, sub2)
        base = pl.multiple_of(r * (sub2 * Q_BLK) + ws, RADIUS)
        o, mb, lb = attend(qm_ref[1, blk], kp_ref[0, pl.ds(base, K_WIN), :],
                           vp_ref[0, pl.ds(base, K_WIN), :], bias_ref[1, var])
        dst = pl.ds(i * (Q_BLK * d2) + r, Q_BLK, stride=d2)
        acc2_ref[dst, :] = o
        m2_ref[dst, :] = mb
        l2_ref[dst, :] = lb
        return carry

    lax.fori_loop(0, n_blk, p2_body, 0)

    d3 = DILATIONS[2]

    def p3_body(r, carry):
        base = pl.multiple_of(r * Q_BLK, Q_BLK)
        o, mb, lb = attend(qm_ref[2, r], kp_ref[1, pl.ds(base, Q_BLK), :],
                           vp_ref[1, pl.ds(base, Q_BLK), :], bias3_ref[...])
        dst = pl.ds(r, Q_BLK, stride=d3)
        acc3_ref[dst, :] = o
        m3_ref[dst, :] = mb
        l3_ref[dst, :] = lb
        return carry

    lax.fori_loop(0, n_blk, p3_body, 0)

    def combine(i, carry):
        rows = pl.ds(pl.multiple_of(i * Q_BLK, Q_BLK), Q_BLK)
        m1, m2, m3 = m1_ref[rows, :], m2_ref[rows, :], m3_ref[rows, :]
        mx = jnp.maximum(jnp.maximum(m1, m2), m3)
        w1, w2, w3 = jnp.exp(m1 - mx), jnp.exp(m2 - mx), jnp.exp(m3 - mx)
        num = w1 * acc1_ref[rows, :] + w2 * acc2_ref[rows, :] + w3 * acc3_ref[rows, :]
        den = w1 * l1_ref[rows, :] + w2 * l2_ref[rows, :] + w3 * l3_ref[rows, :]
        o_ref[0, rows, :] = (num / den).astype(BF16)
        return carry

    lax.fori_loop(0, n_blk, combine, 0)


def _att(q, k, v):
    bsz, s, _ = q.shape
    assert s % (Q_BLK * DILATIONS[2]) == 0 and s // DILATIONS[2] == Q_BLK
    n_blk = s // Q_BLK
    slopes = jnp.asarray(2.0 ** (-8.0 * np.arange(1, N_ATT_HEADS + 1) / N_ATT_HEADS), dtype=F32)
    blk = pl.BlockSpec((1, s, LANES), lambda h, b: (b, 0, h))
    seq = lambda dt: pltpu.VMEM((s, LANES), dt)
    return pl.pallas_call(
        _att_kernel,
        grid=(D_ATT // LANES, bsz),
        in_specs=[pl.BlockSpec(memory_space=pltpu.SMEM), blk, blk, blk],
        out_specs=blk,
        out_shape=jax.ShapeDtypeStruct((bsz, s, D_ATT), BF16),
        scratch_shapes=[pltpu.VMEM((2, 3, 2 * Q_BLK, K_WIN), F32),
                        pltpu.VMEM((2 * Q_BLK, Q_BLK), F32),
                        seq(F32),
                        pltpu.VMEM((3, n_blk, 2 * Q_BLK, LANES), BF16),
                        pltpu.VMEM((2, s, LANES), BF16),
                        pltpu.VMEM((2, s, LANES), BF16)] + [seq(F32)] * 9,
        compiler_params=pltpu.CompilerParams(dimension_semantics=("arbitrary", "arbitrary"),
                                             vmem_limit_bytes=48 * MIB),
        name="att",
    )(slopes, q, k, v)


def _ffn_kernel(x_ref, yc_ref, ya_ref, mod_ref, g_ref, wo_ref, wg_ref, wu_ref, wd_ref,
                o_ref, x1_ref, h_ref, acc_ref):
    gate_m = mod_ref[0, 2:3, :]
    shift = mod_ref[0, 3:4, :]
    scale = mod_ref[0, 4:5, :]
    gate_f = mod_ref[0, 5:6, :]
    mix = _dot(yc_ref[0], wo_ref[0:D_CONV, :]) + _dot(ya_ref[0], wo_ref[D_CONV:, :])
    x1 = x_ref[0] + gate_m * mix
    x1_ref[...] = x1
    r = lax.rsqrt(jnp.mean(x1 * x1, axis=-1, keepdims=True) + EPS)
    h_ref[...] = (((x1 * r) * g_ref[...]) * (1.0 + scale) + shift).astype(BF16)
    acc_ref[...] = jnp.zeros_like(acc_ref)

    def chunk(j, carry):
        h = h_ref[...]
        g = _dot(h, wg_ref[j])
        u = _dot(h, wu_ref[j])
        a = ((g * jax.nn.sigmoid(g)) * u).astype(BF16)
        acc_ref[...] += _dot(a, wd_ref[j])
        return carry

    lax.fori_loop(0, wg_ref.shape[0], chunk, 0)
    o_ref[0] = x1_ref[...] + gate_f * acc_ref[...]


def _ffn(x, yc, ya, mod, g_ffn, w_out, w_gate, w_up, w_down, ts):
    bsz, s, d = x.shape
    n_ch = D_FF // FF_CHUNK
    wg = w_gate.reshape(d, n_ch, FF_CHUNK).transpose(1, 0, 2)
    wu = w_up.reshape(d, n_ch, FF_CHUNK).transpose(1, 0, 2)
    wd = w_down.reshape(n_ch, FF_CHUNK, d)
    const = dict(pipeline_mode=pl.Buffered(1))
    row = lambda b, i: (b, i, 0)
    fix2 = lambda b, i: (0, 0)
    fix3 = lambda b, i: (0, 0, 0)
    return pl.pallas_call(
        _ffn_kernel,
        grid=(bsz, s // ts),
        in_specs=[pl.BlockSpec((1, ts, d), row),
                  pl.BlockSpec((1, ts, D_CONV), row),
                  pl.BlockSpec((1, ts, D_ATT), row),
                  pl.BlockSpec((1, N_MOD, d), lambda b, i: (b, 0, 0)),
                  pl.BlockSpec((1, d), fix2, **const),
                  pl.BlockSpec((d, d), fix2, **const),
                  pl.BlockSpec((n_ch, d, FF_CHUNK), fix3, **const),
                  pl.BlockSpec((n_ch, d, FF_CHUNK), fix3, **const),
                  pl.BlockSpec((n_ch, FF_CHUNK, d), fix3, **const)],
        out_specs=pl.BlockSpec((1, ts, d), row),
        out_shape=jax.ShapeDtypeStruct((bsz, s, d), F32),
        scratch_shapes=[pltpu.VMEM((ts, d), F32), pltpu.VMEM((ts, d), BF16),
                        pltpu.VMEM((ts, d), F32)],
        compiler_params=pltpu.CompilerParams(dimension_semantics=("arbitrary", "arbitrary"),
                                             vmem_limit_bytes=56 * MIB),
        name="ffn",
    )(x, yc, ya, mod, g_ffn.reshape(1, d), w_out, wg, wu, wd)


def kernel(x, c, w_ada, b_ada, g_mix, w_in, w_dw, b_dw, g_conv_ln, b_conv_ln, g_q, g_k,
           w_out, g_ffn, w_gate, w_up, w_down):
    bsz, s, d = x.shape
    for l in range(w_ada.shape[0]):
        mod = _ada(c, w_ada[l], b_ada[l]).reshape(bsz, N_MOD, d)
        u, q, k, v = _inp(x, mod, g_mix[l], w_in[l].astype(BF16), g_q[l], g_k[l], ts=512)
        yc = _conv(u, w_dw[l], b_dw[l], g_conv_ln[l], b_conv_ln[l])
        ya = _att(q, k, v)
        x = _ffn(x, yc, ya, mod, g_ffn[l], w_out[l].astype(BF16), w_gate[l].astype(BF16),
                 w_up[l].astype(BF16), w_down[l].astype(BF16), ts=512)
    return x
```

```python
import functools

import jax
import jax.numpy as jnp
import numpy as np
from jax import lax
from jax.experimental import pallas as pl
from jax.experimental.pallas import tpu as pltpu

F32 = jnp.float32
BF16 = jnp.bfloat16

D_MODEL = 1024
D_CONV = 512
D_ATT = 512
N_ATT_HEADS = 8
HEAD_DIM = 64
CONV_WIDTH = 31
CONV_PAD = CONV_WIDTH // 2
DILATIONS = (1, 4, 16)
RADIUS = 64
D_IN = 2 * D_CONV + 3 * D_ATT
D_FF = 2816
N_MOD = 6
EPS = 1e-6
NEG_INF = -1e30

LANES = 128
Q_BLK = 128
K_WIN = Q_BLK + 2 * RADIUS
MXU_DIM = 256
FF_CHUNK = MXU_DIM
LOG2E = 1.4426950408889634
MIB = 1024 * 1024


def _dot(a, b):
    return jnp.dot(a, b, preferred_element_type=F32)


def _split_bf16(a):
    hi = a.astype(BF16)
    lo = (a - hi.astype(F32)).astype(BF16)
    return hi, lo


def _ada_kernel(c_ref, w_ref, b_ref, o_ref):
    c = c_ref[...]
    a_hi, a_lo = _split_bf16(c * jax.nn.sigmoid(c))
    w_hi, w_lo = _split_bf16(w_ref[...])
    o_ref[...] = _dot(a_hi, w_hi) + _dot(a_hi, w_lo) + _dot(a_lo, w_hi) + b_ref[...]


def _ada(c, w, b):
    bsz, d = c.shape
    n = w.shape[1]
    tn = 1024
    return pl.pallas_call(
        _ada_kernel,
        grid=(n // tn,),
        in_specs=[pl.BlockSpec((bsz, d), lambda j: (0, 0)),
                  pl.BlockSpec((d, tn), lambda j: (0, j)),
                  pl.BlockSpec((1, tn), lambda j: (0, j))],
        out_specs=pl.BlockSpec((bsz, tn), lambda j: (0, j)),
        out_shape=jax.ShapeDtypeStruct((bsz, n), F32),
        compiler_params=pltpu.CompilerParams(dimension_semantics=("arbitrary",),
                                             vmem_limit_bytes=32 * MIB),
        name="ada",
    )(c, w, b.reshape(1, n))


def _head_norm(t, ones, gain):
    sq = (t * t).astype(BF16)
    ssq = jnp.concatenate([_dot(sq[:, c:c + MXU_DIM], ones) for c in range(0, D_ATT, MXU_DIM)],
                          axis=1)
    return t * lax.rsqrt(ssq * (1.0 / HEAD_DIM) + EPS) * gain


def _inp_kernel(x_ref, mod_ref, g_ref, w_ref, ones_ref, gq_ref, gk_ref,
                u_ref, q_ref, k_ref, v_ref):
    x = x_ref[0]
    shift = mod_ref[0, 0:1, :]
    scale = mod_ref[0, 1:2, :]
    r = lax.rsqrt(jnp.mean(x * x, axis=-1, keepdims=True) + EPS)
    h = ((x * r) * g_ref[...]) * (1.0 + scale) + shift
    hb = h.astype(BF16)
    a = _dot(hb, w_ref[:, 0:D_CONV])
    g = _dot(hb, w_ref[:, D_CONV:2 * D_CONV])
    u_ref[0] = (a * jax.nn.sigmoid(g)).astype(BF16)
    ones = ones_ref[...]
    o = 2 * D_CONV
    q = _dot(hb, w_ref[:, o:o + D_ATT])
    q_ref[0] = (_head_norm(q, ones, gq_ref[...]) * (HEAD_DIM ** -0.5 * LOG2E)).astype(BF16)
    k = _dot(hb, w_ref[:, o + D_ATT:o + 2 * D_ATT])
    k_ref[0] = _head_norm(k, ones, gk_ref[...]).astype(BF16)
    v_ref[0] = _dot(hb, w_ref[:, o + 2 * D_ATT:o + 3 * D_ATT]).astype(BF16)


def _inp(x, mod, g_mix, w_in, g_q, g_k, ts):
    bsz, s, d = x.shape
    head = np.arange(MXU_DIM) // HEAD_DIM
    ones = jnp.asarray(head[:, None] == head[None, :], dtype=BF16)
    gq = jnp.tile(g_q, N_ATT_HEADS).reshape(1, D_ATT)
    gk = jnp.tile(g_k, N_ATT_HEADS).reshape(1, D_ATT)
    const = dict(pipeline_mode=pl.Buffered(1))
    row = lambda b, i: (b, i, 0)
    fix = lambda b, i: (0, 0)
    out = jax.ShapeDtypeStruct((bsz, s, D_ATT), BF16)
    return pl.pallas_call(
        _inp_kernel,
        grid=(bsz, s // ts),
        in_specs=[pl.BlockSpec((1, ts, d), row),
                  pl.BlockSpec((1, N_MOD, d), lambda b, i: (b, 0, 0)),
                  pl.BlockSpec((1, d), fix, **const),
                  pl.BlockSpec((d, D_IN), fix, **const),
                  pl.BlockSpec((MXU_DIM, MXU_DIM), fix, **const),
                  pl.BlockSpec((1, D_ATT), fix, **const),
                  pl.BlockSpec((1, D_ATT), fix, **const)],
        out_specs=[pl.BlockSpec((1, ts, D_ATT), row)] * 4,
        out_shape=[out] * 4,
        compiler_params=pltpu.CompilerParams(dimension_semantics=("arbitrary", "arbitrary"),
                                             vmem_limit_bytes=48 * MIB),
        name="inp",
    )(x, mod, g_mix.reshape(1, d), w_in, ones, gq, gk)


def _conv_kernel(u_ref, w_ref, b_ref, g_ref, beta_ref, y_ref, upad_ref, *, rows):
    s = u_ref.shape[1]
    halo = 16
    n_cb = D_CONV // LANES
    zeros = jnp.zeros((halo, LANES), F32)
    for c in range(n_cb):
        upad_ref[c, 0:halo, :] = zeros
        upad_ref[c, halo + s:halo + s + halo, :] = zeros

    def fill(i, carry):
        t0 = pl.multiple_of(i * rows, rows)
        uf = u_ref[0, pl.ds(t0, rows), :].astype(F32)
        for c in range(n_cb):
            upad_ref[c, pl.ds(halo + t0, rows), :] = uf[:, c * LANES:(c + 1) * LANES]
        return carry

    lax.fori_loop(0, s // rows, fill, 0)

    def block(i, carry):
        t0 = pl.multiple_of(i * rows, rows)
        conv = []
        for c in range(n_cb):
            cs = slice(c * LANES, (c + 1) * LANES)
            acc = jnp.broadcast_to(b_ref[:, cs], (rows, LANES))
            for k in range(CONV_WIDTH):
                tap = upad_ref[c, pl.ds(t0 + (halo - CONV_PAD) + k, rows), :]
                acc = acc + w_ref[k:k + 1, cs] * tap
            conv.append(acc)
        tot = conv[0] + conv[1] + conv[2] + conv[3]
        mu = jnp.sum(tot, axis=-1, keepdims=True) * (1.0 / D_CONV)
        cen = [t - mu for t in conv]
        sq = cen[0] * cen[0] + cen[1] * cen[1] + cen[2] * cen[2] + cen[3] * cen[3]
        var = jnp.sum(sq, axis=-1, keepdims=True) * (1.0 / D_CONV)
        rstd = lax.rsqrt(var + EPS)
        for c in range(n_cb):
            cs = slice(c * LANES, (c + 1) * LANES)
            z = (cen[c] * rstd) * g_ref[:, cs] + beta_ref[:, cs]
            y_ref[0, pl.ds(t0, rows), cs] = (z * jax.nn.sigmoid(z)).astype(BF16)
        return carry

    lax.fori_loop(0, s // rows, block, 0, unroll=2)


def _conv(u, w_dw, b_dw, g_ln, b_ln):
    bsz, s, _ = u.shape
    rows = 32
    fix = lambda b: (0, 0)
    vec = pl.BlockSpec((1, D_CONV), fix)
    return pl.pallas_call(
        functools.partial(_conv_kernel, rows=rows),
        grid=(bsz,),
        in_specs=[pl.BlockSpec((1, s, D_CONV), lambda b: (b, 0, 0)),
                  pl.BlockSpec((CONV_WIDTH, D_CONV), fix), vec, vec, vec],
        out_specs=pl.BlockSpec((1, s, D_CONV), lambda b: (b, 0, 0)),
        out_shape=jax.ShapeDtypeStruct((bsz, s, D_CONV), BF16),
        scratch_shapes=[pltpu.VMEM((D_CONV // LANES, s + 32, LANES), F32)],
        compiler_params=pltpu.CompilerParams(dimension_semantics=("arbitrary",),
                                             vmem_limit_bytes=32 * MIB),
        name="conv",
    )(u, w_dw, b_dw.reshape(1, D_CONV), g_ln.reshape(1, D_CONV), b_ln.reshape(1, D_CONV))


def _att_kernel(slopes_ref, q_ref, k_ref, v_ref, o_ref,
                bias_ref, bias3_ref, nat_ref, p4_ref, qp_ref, kp_ref, va_ref,
                acc_ref, m_ref, l_ref):
    hp = pl.program_id(0)
    s = q_ref.shape[1]
    n_blk = s // Q_BLK
    d2, d3 = DILATIONS[1], DILATIONS[2]
    lane = lax.broadcasted_iota(jnp.int32, (1, LANES), 1)
    even = lane < HEAD_DIM

    @pl.when(pl.program_id(1) == 0)
    def _init_tables():
        def table(width, off, dil):
            row = lax.broadcasted_iota(jnp.int32, (2 * Q_BLK, width), 0)
            col = lax.broadcasted_iota(jnp.int32, (2 * Q_BLK, width), 1)
            slope = jnp.where(row < Q_BLK, slopes_ref[2 * hp], slopes_ref[2 * hp + 1])
            dist = jnp.abs(col - (row & (Q_BLK - 1)) - off)
            return jnp.where(dist <= RADIUS, (-LOG2E * slope) * (dil * dist).astype(F32), NEG_INF)

        for var, off in enumerate((0, RADIUS, 2 * RADIUS)):
            for p, dil in enumerate(DILATIONS[:2]):
                bias_ref[p, var] = table(K_WIN, off, dil)
        bias3_ref[...] = table(Q_BLK, 0, d3)
        for p in range(3):
            va_ref[p, :, LANES:2 * LANES] = jnp.ones((s, LANES), BF16)

    n4 = s // d2

    def stage(src, put4, put16):
        nat_ref[...] = src.astype(F32)
        for r in range(d2):
            rows = nat_ref[pl.ds(r, n4, stride=d2), :]
            p4_ref[r * n4:(r + 1) * n4, :] = rows
            put4(r * n4, rows.astype(BF16))
        for r16 in range(d3):
            rows = p4_ref[pl.ds((r16 % d2) * n4 + r16 // d2, Q_BLK, stride=d3 // d2), :]
            put16(r16 * Q_BLK, rows.astype(BF16))

    def put_ref(ref, p):
        def put(t0, val):
            ref[p, t0:t0 + val.shape[0], 0:LANES] = val
        return put

    stage(q_ref[0], put_ref(qp_ref, 0), put_ref(qp_ref, 1))
    stage(k_ref[0], put_ref(kp_ref, 0), put_ref(kp_ref, 1))
    va_ref[0, :, 0:LANES] = v_ref[0]
    stage(v_ref[0], put_ref(va_ref, 1), put_ref(va_ref, 2))

    def attend(qb, kw, vw, bias):
        zero = jnp.zeros_like(qb)
        lhs = jnp.concatenate([jnp.where(even, qb, zero), jnp.where(even, zero, qb)], axis=0)
        sc = lax.dot_general(lhs, kw, (((1,), (1,)), ((), ())), preferred_element_type=F32) + bias
        m = jnp.max(sc, axis=-1, keepdims=True)
        pv = _dot(jnp.exp2(sc - m).astype(BF16), vw)
        o = jnp.where(even, pv[:Q_BLK, :LANES], pv[Q_BLK:, :LANES])
        l = jnp.where(even, pv[:Q_BLK, LANES:], pv[Q_BLK:, LANES:])
        return o, jnp.where(even, m[:Q_BLK], m[Q_BLK:]), l

    def window(i, n_sub):
        ws = jnp.clip(i * Q_BLK - RADIUS, 0, (n_sub - 2) * Q_BLK)
        var = jnp.where(i == 0, 0, jnp.where(i == n_sub - 1, 2, 1))
        return pl.multiple_of(ws, RADIUS), var

    sub2 = n_blk // d2

    def blocks(i, carry):
        t0 = pl.multiple_of(i * Q_BLK, Q_BLK)
        rows = pl.ds(t0, Q_BLK)
        ws, var = window(i, n_blk)
        res1 = attend(q_ref[0, rows, :], k_ref[0, pl.ds(ws, K_WIN), :],
                      va_ref[0, pl.ds(ws, K_WIN), :], bias_ref[0, var])
        r, j = i // sub2, i % sub2
        ws, var = window(j, sub2)
        base = pl.multiple_of(r * (sub2 * Q_BLK) + ws, RADIUS)
        res2 = attend(qp_ref[0, rows, :], kp_ref[0, pl.ds(base, K_WIN), :],
                      va_ref[1, pl.ds(base, K_WIN), :], bias_ref[1, var])
        res3 = attend(qp_ref[1, rows, :], kp_ref[1, rows, :], va_ref[2, rows, :], bias3_ref[...])
        dsts = (rows, pl.ds(j * (Q_BLK * d2) + r, Q_BLK, stride=d2), pl.ds(i, Q_BLK, stride=d3))
        for p, ((o, mb, lb), dst) in enumerate(zip((res1, res2, res3), dsts)):
            acc_ref[p, dst, :] = o
            m_ref[p, dst, :] = mb
            l_ref[p, dst, :] = lb
        return carry

    lax.fori_loop(0, n_blk, blocks, 0, unroll=4)

    def combine(i, carry):
        rows = pl.ds(pl.multiple_of(i * Q_BLK, Q_BLK), Q_BLK)
        m1, m2, m3 = m_ref[0, rows, :], m_ref[1, rows, :], m_ref[2, rows, :]
        mx = jnp.maximum(jnp.maximum(m1, m2), m3)
        w1, w2, w3 = jnp.exp2(m1 - mx), jnp.exp2(m2 - mx), jnp.exp2(m3 - mx)
        num = w1 * acc_ref[0, rows, :] + w2 * acc_ref[1, rows, :] + w3 * acc_ref[2, rows, :]
        den = w1 * l_ref[0, rows, :] + w2 * l_ref[1, rows, :] + w3 * l_ref[2, rows, :]
        o_ref[0, rows, :] = (num / den).astype(BF16)
        return carry

    lax.fori_loop(0, n_blk, combine, 0)


def _att(q, k, v):
    bsz, s, _ = q.shape
    assert s // DILATIONS[2] == Q_BLK and DILATIONS[2] == DILATIONS[1] ** 2
    slopes = jnp.asarray(2.0 ** (-8.0 * np.arange(1, N_ATT_HEADS + 1) / N_ATT_HEADS), dtype=F32)
    blk = pl.BlockSpec((1, s, LANES), lambda h, b: (b, 0, h))
    return pl.pallas_call(
        _att_kernel,
        grid=(D_ATT // LANES, bsz),
        in_specs=[pl.BlockSpec(memory_space=pltpu.SMEM), blk, blk, blk],
        out_specs=blk,
        out_shape=jax.ShapeDtypeStruct((bsz, s, D_ATT), BF16),
        scratch_shapes=[pltpu.VMEM((2, 3, 2 * Q_BLK, K_WIN), F32),
                        pltpu.VMEM((2 * Q_BLK, Q_BLK), F32),
                        pltpu.VMEM((s, LANES), F32),
                        pltpu.VMEM((s, LANES), F32),
                        pltpu.VMEM((2, s, LANES), BF16),
                        pltpu.VMEM((2, s, LANES), BF16),
                        pltpu.VMEM((3, s, 2 * LANES), BF16),
                        pltpu.VMEM((3, s, LANES), F32),
                        pltpu.VMEM((3, s, LANES), F32),
                        pltpu.VMEM((3, s, LANES), F32)],
        compiler_params=pltpu.CompilerParams(dimension_semantics=("arbitrary", "arbitrary"),
                                             vmem_limit_bytes=48 * MIB),
        name="att",
    )(slopes, q, k, v)


def _ffn_kernel(x_ref, yc_ref, ya_ref, mod_ref, g_ref, wo_ref, wg_ref, wu_ref, wd_ref,
                o_ref, x1_ref, h_ref, acc_ref):
    gate_m = mod_ref[0, 2:3, :]
    shift = mod_ref[0, 3:4, :]
    scale = mod_ref[0, 4:5, :]
    gate_f = mod_ref[0, 5:6, :]
    mix = _dot(yc_ref[0], wo_ref[0:D_CONV, :]) + _dot(ya_ref[0], wo_ref[D_CONV:, :])
    x1 = x_ref[0] + gate_m * mix
    x1_ref[...] = x1
    r = lax.rsqrt(jnp.mean(x1 * x1, axis=-1, keepdims=True) + EPS)
    h_ref[...] = (((x1 * r) * g_ref[...]) * (1.0 + scale) + shift).astype(BF16)
    acc_ref[...] = jnp.zeros_like(acc_ref)

    def chunk(j, carry):
        h = h_ref[...]
        g = _dot(h, wg_ref[j])
        u = _dot(h, wu_ref[j])
        a = ((g * jax.nn.sigmoid(g)) * u).astype(BF16)
        acc_ref[...] += _dot(a, wd_ref[j])
        return carry

    lax.fori_loop(0, wg_ref.shape[0], chunk, 0, unroll=True)
    o_ref[0] = x1_ref[...] + gate_f * acc_ref[...]


def _ffn(x, yc, ya, mod, g_ffn, w_out, w_gate, w_up, w_down, ts):
    bsz, s, d = x.shape
    n_ch = D_FF // FF_CHUNK
    wg = w_gate.reshape(d, n_ch, FF_CHUNK).transpose(1, 0, 2)
    wu = w_up.reshape(d, n_ch, FF_CHUNK).transpose(1, 0, 2)
    wd = w_down.reshape(n_ch, FF_CHUNK, d)
    const = dict(pipeline_mode=pl.Buffered(1))
    row = lambda b, i: (b, i, 0)
    fix2 = lambda b, i: (0, 0)
    fix3 = lambda b, i: (0, 0, 0)
    return pl.pallas_call(
        _ffn_kernel,
        grid=(bsz, s // ts),
        in_specs=[pl.BlockSpec((1, ts, d), row),
                  pl.BlockSpec((1, ts, D_CONV), row),
                  pl.BlockSpec((1, ts, D_ATT), row),
                  pl.BlockSpec((1, N_MOD, d), lambda b, i: (b, 0, 0)),
                  pl.BlockSpec((1, d), fix2, **const),
                  pl.BlockSpec((d, d), fix2, **const),
                  pl.BlockSpec((n_ch, d, FF_CHUNK), fix3, **const),
                  pl.BlockSpec((n_ch, d, FF_CHUNK), fix3, **const),
                  pl.BlockSpec((n_ch, FF_CHUNK, d), fix3, **const)],
        out_specs=pl.BlockSpec((1, ts, d), row),
        out_shape=jax.ShapeDtypeStruct((bsz, s, d), F32),
        scratch_shapes=[pltpu.VMEM((ts, d), F32), pltpu.VMEM((ts, d), BF16),
                        pltpu.VMEM((ts, d), F32)],
        compiler_params=pltpu.CompilerParams(dimension_semantics=("arbitrary", "arbitrary"),
                                             vmem_limit_bytes=56 * MIB),
        name="ffn",
    )(x, yc, ya, mod, g_ffn.reshape(1, d), w_out, wg, wu, wd)


def kernel(x, c, w_ada, b_ada, g_mix, w_in, w_dw, b_dw, g_conv_ln, b_conv_ln, g_q, g_k,
           w_out, g_ffn, w_gate, w_up, w_down):
    bsz, s, d = x.shape
    for l in range(w_ada.shape[0]):
        mod = _ada(c, w_ada[l], b_ada[l]).reshape(bsz, N_MOD, d)
        u, q, k, v = _inp(x, mod, g_mix[l], w_in[l].astype(BF16), g_q[l], g_k[l], ts=512)
        yc = _conv(u, w_dw[l], b_dw[l], g_conv_ln[l], b_conv_ln[l])
        ya = _att(q, k, v)
        x = _ffn(x, yc, ya, mod, g_ffn[l], w_out[l].astype(BF16), w_gate[l].astype(BF16),
                 w_up[l].astype(BF16), w_down[l].astype(BF16), ts=512)
    return x
```

```python
import functools

import jax
import jax.numpy as jnp
import numpy as np
from jax import lax
from jax.experimental import pallas as pl
from jax.experimental.pallas import tpu as pltpu

F32 = jnp.float32
BF16 = jnp.bfloat16

D_MODEL = 1024
D_CONV = 512
D_ATT = 512
N_ATT_HEADS = 8
HEAD_DIM = 64
CONV_WIDTH = 31
CONV_PAD = CONV_WIDTH // 2
DILATIONS = (1, 4, 16)
RADIUS = 64
D_IN = 2 * D_CONV + 3 * D_ATT
D_FF = 2816
N_MOD = 6
EPS = 1e-6
NEG_INF = -1e30

LANES = 128
Q_BLK = 128
K_WIN = Q_BLK + 2 * RADIUS
MXU_DIM = 256
FF_CHUNK = MXU_DIM
LOG2E = 1.4426950408889634
MIB = 1024 * 1024


def _dot(a, b):
    return jnp.dot(a, b, preferred_element_type=F32)


def _split_bf16(a):
    hi = a.astype(BF16)
    lo = (a - hi.astype(F32)).astype(BF16)
    return hi, lo


def _ada_kernel(c_ref, w_ref, b_ref, o_ref):
    c = c_ref[...]
    a_hi, a_lo = _split_bf16(c * jax.nn.sigmoid(c))
    w_hi, w_lo = _split_bf16(w_ref[...])
    o_ref[...] = _dot(a_hi, w_hi) + _dot(a_hi, w_lo) + _dot(a_lo, w_hi) + b_ref[...]


def _ada(c, w, b):
    bsz, d = c.shape
    n = w.shape[1]
    tn = 1024
    return pl.pallas_call(
        _ada_kernel,
        grid=(n // tn,),
        in_specs=[pl.BlockSpec((bsz, d), lambda j: (0, 0)),
                  pl.BlockSpec((d, tn), lambda j: (0, j)),
                  pl.BlockSpec((1, tn), lambda j: (0, j))],
        out_specs=pl.BlockSpec((bsz, tn), lambda j: (0, j)),
        out_shape=jax.ShapeDtypeStruct((bsz, n), F32),
        compiler_params=pltpu.CompilerParams(dimension_semantics=("arbitrary",),
                                             vmem_limit_bytes=32 * MIB),
        name="ada",
    )(c, w, b.reshape(1, n))


def _head_norm(t, ones, gain):
    sq = (t * t).astype(BF16)
    ssq = jnp.concatenate([_dot(sq[:, c:c + MXU_DIM], ones) for c in range(0, D_ATT, MXU_DIM)],
                          axis=1)
    return t * lax.rsqrt(ssq * (1.0 / HEAD_DIM) + EPS) * gain


def _inp_kernel(x_ref, mod_ref, g_ref, w_ref, ones_ref, gq_ref, gk_ref,
                u_ref, q_ref, k_ref, v_ref):
    x = x_ref[0]
    shift = mod_ref[0, 0:1, :]
    scale = mod_ref[0, 1:2, :]
    r = lax.rsqrt(jnp.mean(x * x, axis=-1, keepdims=True) + EPS)
    h = ((x * r) * g_ref[...]) * (1.0 + scale) + shift
    hb = h.astype(BF16)
    a = _dot(hb, w_ref[:, 0:D_CONV])
    g = _dot(hb, w_ref[:, D_CONV:2 * D_CONV])
    u_ref[0] = (a * jax.nn.sigmoid(g)).astype(BF16)
    ones = ones_ref[...]
    o = 2 * D_CONV
    q = _dot(hb, w_ref[:, o:o + D_ATT])
    q_ref[0] = (_head_norm(q, ones, gq_ref[...]) * (HEAD_DIM ** -0.5 * LOG2E)).astype(BF16)
    k = _dot(hb, w_ref[:, o + D_ATT:o + 2 * D_ATT])
    k_ref[0] = _head_norm(k, ones, gk_ref[...]).astype(BF16)
    v_ref[0] = _dot(hb, w_ref[:, o + 2 * D_ATT:o + 3 * D_ATT]).astype(BF16)


def _inp(x, mod, g_mix, w_in, g_q, g_k, ts):
    bsz, s, d = x.shape
    head = np.arange(MXU_DIM) // HEAD_DIM
    ones = jnp.asarray(head[:, None] == head[None, :], dtype=BF16)
    gq = jnp.tile(g_q, N_ATT_HEADS).reshape(1, D_ATT)
    gk = jnp.tile(g_k, N_ATT_HEADS).reshape(1, D_ATT)
    const = dict(pipeline_mode=pl.Buffered(1))
    row = lambda b, i: (b, i, 0)
    fix = lambda b, i: (0, 0)
    out = jax.ShapeDtypeStruct((bsz, s, D_ATT), BF16)
    return pl.pallas_call(
        _inp_kernel,
        grid=(bsz, s // ts),
        in_specs=[pl.BlockSpec((1, ts, d), row),
                  pl.BlockSpec((1, N_MOD, d), lambda b, i: (b, 0, 0)),
                  pl.BlockSpec((1, d), fix, **const),
                  pl.BlockSpec((d, D_IN), fix, **const),
                  pl.BlockSpec((MXU_DIM, MXU_DIM), fix, **const),
                  pl.BlockSpec((1, D_ATT), fix, **const),
                  pl.BlockSpec((1, D_ATT), fix, **const)],
        out_specs=[pl.BlockSpec((1, ts, D_ATT), row)] * 4,
        out_shape=[out] * 4,
        compiler_params=pltpu.CompilerParams(dimension_semantics=("arbitrary", "arbitrary"),
                                             vmem_limit_bytes=48 * MIB),
        name="inp",
    )(x, mod, g_mix.reshape(1, d), w_in, ones, gq, gk)


def _conv_kernel(u_ref, w_ref, b_ref, g_ref, beta_ref, y_ref, upad_ref, cv_ref, *, rows):
    s = u_ref.shape[1]
    halo = 16
    n_cb = D_CONV // LANES
    zeros = jnp.zeros((halo, LANES), F32)
    for c in range(n_cb):
        upad_ref[c, 0:halo, :] = zeros
        upad_ref[c, halo + s:halo + s + halo, :] = zeros

    def fill(i, carry):
        t0 = pl.multiple_of(i * rows, rows)
        uf = u_ref[0, pl.ds(t0, rows), :].astype(F32)
        for c in range(n_cb):
            upad_ref[c, pl.ds(halo + t0, rows), :] = uf[:, c * LANES:(c + 1) * LANES]
        return carry

    lax.fori_loop(0, s // rows, fill, 0)

    def taps(i, carry):
        t0 = pl.multiple_of(i * rows, rows)
        for c in range(n_cb):
            cs = slice(c * LANES, (c + 1) * LANES)
            acc = jnp.broadcast_to(b_ref[:, cs], (rows, LANES))
            for k in range(CONV_WIDTH):
                tap = upad_ref[c, pl.ds(t0 + (halo - CONV_PAD) + k, rows), :]
                acc = acc + w_ref[k:k + 1, cs] * tap
            cv_ref[c, pl.ds(t0, rows), :] = acc
        return carry

    lax.fori_loop(0, s // rows, taps, 0)

    def block(i, carry):
        t0 = pl.multiple_of(i * rows, rows)
        conv = [cv_ref[c, pl.ds(t0, rows), :] for c in range(n_cb)]
        tot = conv[0] + conv[1] + conv[2] + conv[3]
        mu = jnp.sum(tot, axis=-1, keepdims=True) * (1.0 / D_CONV)
        cen = [t - mu for t in conv]
        sq = cen[0] * cen[0] + cen[1] * cen[1] + cen[2] * cen[2] + cen[3] * cen[3]
        var = jnp.sum(sq, axis=-1, keepdims=True) * (1.0 / D_CONV)
        rstd = lax.rsqrt(var + EPS)
        for c in range(n_cb):
            cs = slice(c * LANES, (c + 1) * LANES)
            z = (cen[c] * rstd) * g_ref[:, cs] + beta_ref[:, cs]
            y_ref[0, pl.ds(t0, rows), cs] = (z * jax.nn.sigmoid(z)).astype(BF16)
        return carry

    lax.fori_loop(0, s // rows, block, 0, unroll=4)


def _conv(u, w_dw, b_dw, g_ln, b_ln):
    bsz, s, _ = u.shape
    rows = 64
    fix = lambda b: (0, 0)
    vec = pl.BlockSpec((1, D_CONV), fix)
    return pl.pallas_call(
        functools.partial(_conv_kernel, rows=rows),
        grid=(bsz,),
        in_specs=[pl.BlockSpec((1, s, D_CONV), lambda b: (b, 0, 0)),
                  pl.BlockSpec((CONV_WIDTH, D_CONV), fix), vec, vec, vec],
        out_specs=pl.BlockSpec((1, s, D_CONV), lambda b: (b, 0, 0)),
        out_shape=jax.ShapeDtypeStruct((bsz, s, D_CONV), BF16),
        scratch_shapes=[pltpu.VMEM((D_CONV // LANES, s + 32, LANES), F32),
                        pltpu.VMEM((D_CONV // LANES, s, LANES), F32)],
        compiler_params=pltpu.CompilerParams(dimension_semantics=("arbitrary",),
                                             vmem_limit_bytes=32 * MIB),
        name="conv",
    )(u, w_dw, b_dw.reshape(1, D_CONV), g_ln.reshape(1, D_CONV), b_ln.reshape(1, D_CONV))


def _att_kernel(slopes_ref, q_ref, k_ref, v_ref, o_ref,
                bias_ref, bias3_ref, nat_ref, p4_ref, qp_ref, kp_ref, va_ref,
                s_ref, p_ref, acc_ref, m_ref, l_ref):
    hp = pl.program_id(0)
    s = q_ref.shape[1]
    n_blk = s // Q_BLK
    d2, d3 = DILATIONS[1], DILATIONS[2]
    lane = lax.broadcasted_iota(jnp.int32, (1, LANES), 1)
    even = lane < HEAD_DIM

    @pl.when(pl.program_id(1) == 0)
    def _init_tables():
        def table(width, off, dil):
            row = lax.broadcasted_iota(jnp.int32, (2 * Q_BLK, width), 0)
            col = lax.broadcasted_iota(jnp.int32, (2 * Q_BLK, width), 1)
            slope = jnp.where(row < Q_BLK, slopes_ref[2 * hp], slopes_ref[2 * hp + 1])
            dist = jnp.abs(col - (row & (Q_BLK - 1)) - off)
            return jnp.where(dist <= RADIUS, (-LOG2E * slope) * (dil * dist).astype(F32), NEG_INF)

        for var, off in enumerate((0, RADIUS, 2 * RADIUS)):
            for p, dil in enumerate(DILATIONS[:2]):
                bias_ref[p, var] = table(K_WIN, off, dil)
        bias3_ref[...] = table(Q_BLK, 0, d3)
        for p in range(3):
            va_ref[p, :, LANES:2 * LANES] = jnp.ones((s, LANES), BF16)

    n4 = s // d2

    def stage(src, put4, put16):
        nat_ref[...] = src.astype(F32)
        for r in range(d2):
            rows = nat_ref[pl.ds(r, n4, stride=d2), :]
            p4_ref[r * n4:(r + 1) * n4, :] = rows
            put4(r * n4, rows.astype(BF16))
        for r16 in range(d3):
            rows = p4_ref[pl.ds((r16 % d2) * n4 + r16 // d2, Q_BLK, stride=d3 // d2), :]
            put16(r16 * Q_BLK, rows.astype(BF16))

    def put_ref(ref, p):
        def put(t0, val):
            ref[p, t0:t0 + val.shape[0], 0:LANES] = val
        return put

    stage(q_ref[0], put_ref(qp_ref, 0), put_ref(qp_ref, 1))
    stage(k_ref[0], put_ref(kp_ref, 0), put_ref(kp_ref, 1))
    va_ref[0, :, 0:LANES] = v_ref[0]
    stage(v_ref[0], put_ref(va_ref, 1), put_ref(va_ref, 2))

    sub2 = n_blk // d2

    def window(i, n_sub):
        if isinstance(i, int):
            return (min(max(i * Q_BLK - RADIUS, 0), (n_sub - 2) * Q_BLK),
                    0 if i == 0 else (2 if i == n_sub - 1 else 1))
        ws = jnp.clip(i * Q_BLK - RADIUS, 0, (n_sub - 2) * Q_BLK)
        var = jnp.where(i == 0, 0, jnp.where(i == n_sub - 1, 2, 1))
        return pl.multiple_of(ws, RADIUS), var

    def place(i):
        aligned = (lambda v, m: v) if isinstance(i, int) else pl.multiple_of
        rows = pl.ds(aligned(i * Q_BLK, Q_BLK), Q_BLK)
        ws1, var1 = window(i, n_blk)
        r, j = i // sub2, i % sub2
        ws2, var2 = window(j, sub2)
        base2 = aligned(r * (sub2 * Q_BLK) + ws2, RADIUS)
        wins = (pl.ds(ws1, K_WIN), pl.ds(base2, K_WIN), rows)
        dsts = (rows, pl.ds(j * (Q_BLK * d2) + r, Q_BLK, stride=d2), pl.ds(i, Q_BLK, stride=d3))
        return rows, wins, (var1, var2), dsts

    widths = (K_WIN, K_WIN, Q_BLK)

    def scores(i):
        rows, wins, vars_, _ = place(i)
        qs = (q_ref[0, rows, :], qp_ref[0, rows, :], qp_ref[1, rows, :])
        ks = (k_ref[0, wins[0], :], kp_ref[0, wins[1], :], kp_ref[1, wins[2], :])
        biases = (bias_ref[0, vars_[0]], bias_ref[1, vars_[1]], bias3_ref[...])
        for p in range(3):
            zero = jnp.zeros_like(qs[p])
            lhs = jnp.concatenate([jnp.where(even, qs[p], zero), jnp.where(even, zero, qs[p])],
                                  axis=0)
            sc = lax.dot_general(lhs, ks[p], (((1,), (1,)), ((), ())),
                                 preferred_element_type=F32)
            s_ref[p, :, 0:widths[p]] = sc + biases[p]

    def softmax(i):
        _, _, _, dsts = place(i)
        for p in range(3):
            sc = s_ref[p, :, 0:widths[p]]
            m = jnp.max(sc, axis=-1, keepdims=True)
            p_ref[p, :, 0:widths[p]] = jnp.exp2(sc - m).astype(BF16)
            m_ref[p, dsts[p], :] = jnp.where(even, m[:Q_BLK], m[Q_BLK:])

    def values(i):
        _, wins, _, dsts = place(i)
        for p in range(3):
            pv = _dot(p_ref[p, :, 0:widths[p]], va_ref[p, wins[p], :])
            acc_ref[p, dsts[p], :] = jnp.where(even, pv[:Q_BLK, :LANES], pv[Q_BLK:, :LANES])
            l_ref[p, dsts[p], :] = jnp.where(even, pv[:Q_BLK, LANES:], pv[Q_BLK:, LANES:])

    scores(0)
    softmax(0)
    scores(1)

    def steady(i, carry):
        values(i)
        softmax(i + 1)
        scores(i + 2)
        return carry

    lax.fori_loop(0, n_blk - 2, steady, 0, unroll=7)
    values(n_blk - 2)
    softmax(n_blk - 1)
    values(n_blk - 1)

    def combine(i, carry):
        rows = pl.ds(pl.multiple_of(i * Q_BLK, Q_BLK), Q_BLK)
        m1, m2, m3 = m_ref[0, rows, :], m_ref[1, rows, :], m_ref[2, rows, :]
        mx = jnp.maximum(jnp.maximum(m1, m2), m3)
        w1, w2, w3 = jnp.exp2(m1 - mx), jnp.exp2(m2 - mx), jnp.exp2(m3 - mx)
        num = w1 * acc_ref[0, rows, :] + w2 * acc_ref[1, rows, :] + w3 * acc_ref[2, rows, :]
        den = w1 * l_ref[0, rows, :] + w2 * l_ref[1, rows, :] + w3 * l_ref[2, rows, :]
        o_ref[0, rows, :] = (num / den).astype(BF16)
        return carry

    lax.fori_loop(0, n_blk, combine, 0)


def _att(q, k, v):
    bsz, s, _ = q.shape
    assert s // DILATIONS[2] == Q_BLK and DILATIONS[2] == DILATIONS[1] ** 2
    slopes = jnp.asarray(2.0 ** (-8.0 * np.arange(1, N_ATT_HEADS + 1) / N_ATT_HEADS), dtype=F32)
    blk = pl.BlockSpec((1, s, LANES), lambda h, b: (b, 0, h))
    return pl.pallas_call(
        _att_kernel,
        grid=(D_ATT // LANES, bsz),
        in_specs=[pl.BlockSpec(memory_space=pltpu.SMEM), blk, blk, blk],
        out_specs=blk,
        out_shape=jax.ShapeDtypeStruct((bsz, s, D_ATT), BF16),
        scratch_shapes=[pltpu.VMEM((2, 3, 2 * Q_BLK, K_WIN), F32),
                        pltpu.VMEM((2 * Q_BLK, Q_BLK), F32),
                        pltpu.VMEM((s, LANES), F32),
                        pltpu.VMEM((s, LANES), F32),
                        pltpu.VMEM((2, s, LANES), BF16),
                        pltpu.VMEM((2, s, LANES), BF16),
                        pltpu.VMEM((3, s, 2 * LANES), BF16),
                        pltpu.VMEM((3, 2 * Q_BLK, K_WIN), F32),
                        pltpu.VMEM((3, 2 * Q_BLK, K_WIN), BF16),
                        pltpu.VMEM((3, s, LANES), F32),
                        pltpu.VMEM((3, s, LANES), F32),
                        pltpu.VMEM((3, s, LANES), F32)],
        compiler_params=pltpu.CompilerParams(dimension_semantics=("arbitrary", "arbitrary"),
                                             vmem_limit_bytes=48 * MIB),
        name="att",
    )(slopes, q, k, v)


def _ffn_kernel(x_ref, yc_ref, ya_ref, mod_ref, g_ref, wo_ref, wg_ref, wu_ref, wd_ref,
                o_ref, x1_ref, h_ref, acc_ref):
    gate_m = mod_ref[0, 2:3, :]
    shift = mod_ref[0, 3:4, :]
    scale = mod_ref[0, 4:5, :]
    gate_f = mod_ref[0, 5:6, :]
    mix = _dot(yc_ref[0], wo_ref[0:D_CONV, :]) + _dot(ya_ref[0], wo_ref[D_CONV:, :])
    x1 = x_ref[0] + gate_m * mix
    x1_ref[...] = x1
    r = lax.rsqrt(jnp.mean(x1 * x1, axis=-1, keepdims=True) + EPS)
    h_ref[...] = (((x1 * r) * g_ref[...]) * (1.0 + scale) + shift).astype(BF16)
    acc_ref[...] = jnp.zeros_like(acc_ref)

    def chunk(j, carry):
        h = h_ref[...]
        g = _dot(h, wg_ref[j])
        u = _dot(h, wu_ref[j])
        a = ((g * jax.nn.sigmoid(g)) * u).astype(BF16)
        acc_ref[...] += _dot(a, wd_ref[j])
        return carry

    lax.fori_loop(0, wg_ref.shape[0], chunk, 0, unroll=True)
    o_ref[0] = x1_ref[...] + gate_f * acc_ref[...]


def _ffn(x, yc, ya, mod, g_ffn, w_out, w_gate, w_up, w_down, ts):
    bsz, s, d = x.shape
    n_ch = D_FF // FF_CHUNK
    wg = w_gate.reshape(d, n_ch, FF_CHUNK).transpose(1, 0, 2)
    wu = w_up.reshape(d, n_ch, FF_CHUNK).transpose(1, 0, 2)
    wd = w_down.reshape(n_ch, FF_CHUNK, d)
    const = dict(pipeline_mode=pl.Buffered(1))
    row = lambda b, i: (b, i, 0)
    fix2 = lambda b, i: (0, 0)
    fix3 = lambda b, i: (0, 0, 0)
    return pl.pallas_call(
        _ffn_kernel,
        grid=(bsz, s // ts),
        in_specs=[pl.BlockSpec((1, ts, d), row),
                  pl.BlockSpec((1, ts, D_CONV), row),
                  pl.BlockSpec((1, ts, D_ATT), row),
                  pl.BlockSpec((1, N_MOD, d), lambda b, i: (b, 0, 0)),
                  pl.BlockSpec((1, d), fix2, **const),
                  pl.BlockSpec((d, d), fix2, **const),
                  pl.BlockSpec((n_ch, d, FF_CHUNK), fix3, **const),
                  pl.BlockSpec((n_ch, d, FF_CHUNK), fix3, **const),
                  pl.BlockSpec((n_ch, FF_CHUNK, d), fix3, **const)],
        out_specs=pl.BlockSpec((1, ts, d), row),
        out_shape=jax.ShapeDtypeStruct((bsz, s, d), F32),
        scratch_shapes=[pltpu.VMEM((ts, d), F32), pltpu.VMEM((ts, d), BF16),
                        pltpu.VMEM((ts, d), F32)],
        compiler_params=pltpu.CompilerParams(dimension_semantics=("arbitrary", "arbitrary"),
                                             vmem_limit_bytes=56 * MIB),
        name="ffn",
    )(x, yc, ya, mod, g_ffn.reshape(1, d), w_out, wg, wu, wd)


def kernel(x, c, w_ada, b_ada, g_mix, w_in, w_dw, b_dw, g_conv_ln, b_conv_ln, g_q, g_k,
           w_out, g_ffn, w_gate, w_up, w_down):
    bsz, s, d = x.shape
    for l in range(w_ada.shape[0]):
        mod = _ada(c, w_ada[l], b_ada[l]).reshape(bsz, N_MOD, d)
        u, q, k, v = _inp(x, mod, g_mix[l], w_in[l].astype(BF16), g_q[l], g_k[l], ts=512)
        yc = _conv(u, w_dw[l], b_dw[l], g_conv_ln[l], b_conv_ln[l])
        ya = _att(q, k, v)
        x = _ffn(x, yc, ya, mod, g_ffn[l], w_out[l].astype(BF16), w_gate[l].astype(BF16),
                 w_up[l].astype(BF16), w_down[l].astype(BF16), ts=512)
    return x
```

```python
import functools

import jax
import jax.numpy as jnp
import numpy as np
from jax import lax
from jax.experimental import pallas as pl
from jax.experimental.pallas import tpu as pltpu

F32 = jnp.float32
BF16 = jnp.bfloat16

D_MODEL = 1024
D_CONV = 512
D_ATT = 512
N_ATT_HEADS = 8
HEAD_DIM = 64
CONV_WIDTH = 31
CONV_PAD = CONV_WIDTH // 2
DILATIONS = (1, 4, 16)
RADIUS = 64
D_IN = 2 * D_CONV + 3 * D_ATT
D_FF = 2816
N_MOD = 6
EPS = 1e-6
NEG_INF = -1e30
LOG2E = 1.4426950408889634

LANES = 128
MXU_DIM = 256
MIB = 1024 * 1024
TOKEN_TILE = 512
Q_BLK = 128
K_WIN = Q_BLK + 2 * RADIUS
FF_CHUNK = MXU_DIM
CONV_HALO = 16
CONV_ROWS = 64
MAX_SOFTMAX_SPAN = 120.0


def _dot(a, b):
    return jnp.dot(a, b, preferred_element_type=F32)


def _split_bf16(a):
    hi = a.astype(BF16)
    lo = (a - hi.astype(F32)).astype(BF16)
    return hi, lo


def _ada_kernel(c_ref, w_ref, b_ref, o_ref):
    c = c_ref[...]
    a_hi, a_lo = _split_bf16(c * jax.nn.sigmoid(c))
    w_hi, w_lo = _split_bf16(w_ref[...])
    o_ref[...] = _dot(a_hi, w_hi) + _dot(a_hi, w_lo) + _dot(a_lo, w_hi) + b_ref[...]


def _ada(c, w, b):
    bsz, d = c.shape
    n = w.shape[1]
    tn = 1024
    return pl.pallas_call(
        _ada_kernel,
        grid=(n // tn,),
        in_specs=[pl.BlockSpec((bsz, d), lambda j: (0, 0)),
                  pl.BlockSpec((d, tn), lambda j: (0, j)),
                  pl.BlockSpec((1, tn), lambda j: (0, j))],
        out_specs=pl.BlockSpec((bsz, tn), lambda j: (0, j)),
        out_shape=jax.ShapeDtypeStruct((bsz, n), F32),
        compiler_params=pltpu.CompilerParams(dimension_semantics=("arbitrary",),
                                             vmem_limit_bytes=32 * MIB),
        name="ada",
    )(c, w, b.reshape(1, n))


def _head_norm(t, ones, gain):
    sq = (t * t).astype(BF16)
    ssq = jnp.concatenate([_dot(sq[:, c:c + MXU_DIM], ones) for c in range(0, D_ATT, MXU_DIM)],
                          axis=1)
    return t * lax.rsqrt(ssq * (1.0 / HEAD_DIM) + EPS) * gain


def _inp_kernel(x_ref, mod_ref, g_ref, w_ref, ones_ref, gq_ref, gk_ref,
                u_ref, q_ref, k_ref, v_ref):
    x = x_ref[0]
    shift = mod_ref[0, 0:1, :]
    scale = mod_ref[0, 1:2, :]
    r = lax.rsqrt(jnp.mean(x * x, axis=-1, keepdims=True) + EPS)
    h = ((x * r) * g_ref[...]) * (1.0 + scale) + shift
    hb = h.astype(BF16)
    a = _dot(hb, w_ref[:, 0:D_CONV])
    g = _dot(hb, w_ref[:, D_CONV:2 * D_CONV])
    u_ref[0] = (a * jax.nn.sigmoid(g)).astype(BF16)
    ones = ones_ref[...]
    o = 2 * D_CONV
    q = _dot(hb, w_ref[:, o:o + D_ATT])
    q_ref[0] = (_head_norm(q, ones, gq_ref[...]) * (HEAD_DIM ** -0.5 * LOG2E)).astype(BF16)
    k = _dot(hb, w_ref[:, o + D_ATT:o + 2 * D_ATT])
    k_ref[0] = _head_norm(k, ones, gk_ref[...]).astype(BF16)
    v_ref[0] = _dot(hb, w_ref[:, o + 2 * D_ATT:o + 3 * D_ATT]).astype(BF16)


def _inp(x, mod, g_mix, w_in, g_q, g_k):
    bsz, s, d = x.shape
    ts = TOKEN_TILE
    head = np.arange(MXU_DIM) // HEAD_DIM
    ones = jnp.asarray(head[:, None] == head[None, :], dtype=BF16)
    gq = jnp.tile(g_q, N_ATT_HEADS).reshape(1, D_ATT)
    gk = jnp.tile(g_k, N_ATT_HEADS).reshape(1, D_ATT)
    const = dict(pipeline_mode=pl.Buffered(1))
    row = lambda b, i: (b, i, 0)
    fix = lambda b, i: (0, 0)
    out = jax.ShapeDtypeStruct((bsz, s, D_ATT), BF16)
    return pl.pallas_call(
        _inp_kernel,
        grid=(bsz, s // ts),
        in_specs=[pl.BlockSpec((1, ts, d), row),
                  pl.BlockSpec((1, N_MOD, d), lambda b, i: (b, 0, 0)),
                  pl.BlockSpec((1, d), fix, **const),
                  pl.BlockSpec((d, D_IN), fix, **const),
                  pl.BlockSpec((MXU_DIM, MXU_DIM), fix, **const),
                  pl.BlockSpec((1, D_ATT), fix, **const),
                  pl.BlockSpec((1, D_ATT), fix, **const)],
        out_specs=[pl.BlockSpec((1, ts, D_ATT), row)] * 4,
        out_shape=[out] * 4,
        compiler_params=pltpu.CompilerParams(dimension_semantics=("arbitrary", "arbitrary"),
                                             vmem_limit_bytes=48 * MIB),
        name="inp",
    )(x, mod, g_mix.reshape(1, d), w_in, ones, gq, gk)


def _conv_kernel(u_ref, w_ref, b_ref, g_ref, beta_ref, y_ref, upad_ref, cv_ref):
    s = u_ref.shape[1]
    rows = CONV_ROWS
    n_cb = D_CONV // LANES
    zeros = jnp.zeros((CONV_HALO, LANES), F32)
    for c in range(n_cb):
        upad_ref[c, 0:CONV_HALO, :] = zeros
        upad_ref[c, CONV_HALO + s:CONV_HALO + s + CONV_HALO, :] = zeros

    def fill(i, carry):
        t0 = pl.multiple_of(i * rows, rows)
        uf = u_ref[0, pl.ds(t0, rows), :].astype(F32)
        for c in range(n_cb):
            upad_ref[c, pl.ds(CONV_HALO + t0, rows), :] = uf[:, c * LANES:(c + 1) * LANES]
        return carry

    lax.fori_loop(0, s // rows, fill, 0)

    def taps(i, carry):
        t0 = pl.multiple_of(i * rows, rows)
        for c in range(n_cb):
            cs = slice(c * LANES, (c + 1) * LANES)
            acc = jnp.broadcast_to(b_ref[:, cs], (rows, LANES))
            for k in range(CONV_WIDTH):
                tap = upad_ref[c, pl.ds(t0 + (CONV_HALO - CONV_PAD) + k, rows), :]
                acc = acc + w_ref[k:k + 1, cs] * tap
            cv_ref[c, pl.ds(t0, rows), :] = acc
        return carry

    lax.fori_loop(0, s // rows, taps, 0)

    def norm(i, carry):
        t0 = pl.multiple_of(i * rows, rows)
        conv = [cv_ref[c, pl.ds(t0, rows), :] for c in range(n_cb)]
        tot = conv[0] + conv[1] + conv[2] + conv[3]
        mu = jnp.sum(tot, axis=-1, keepdims=True) * (1.0 / D_CONV)
        cen = [t - mu for t in conv]
        sq = cen[0] * cen[0] + cen[1] * cen[1] + cen[2] * cen[2] + cen[3] * cen[3]
        var = jnp.sum(sq, axis=-1, keepdims=True) * (1.0 / D_CONV)
        rstd = lax.rsqrt(var + EPS)
        for c in range(n_cb):
            cs = slice(c * LANES, (c + 1) * LANES)
            z = (cen[c] * rstd) * g_ref[:, cs] + beta_ref[:, cs]
            y_ref[0, pl.ds(t0, rows), cs] = (z * jax.nn.sigmoid(z)).astype(BF16)
        return carry

    lax.fori_loop(0, s // rows, norm, 0, unroll=4)


def _conv(u, w_dw, b_dw, g_ln, b_ln):
    bsz, s, _ = u.shape
    fix = lambda b: (0, 0)
    vec = pl.BlockSpec((1, D_CONV), fix)
    return pl.pallas_call(
        _conv_kernel,
        grid=(bsz,),
        in_specs=[pl.BlockSpec((1, s, D_CONV), lambda b: (b, 0, 0)),
                  pl.BlockSpec((CONV_WIDTH, D_CONV), fix), vec, vec, vec],
        out_specs=pl.BlockSpec((1, s, D_CONV), lambda b: (b, 0, 0)),
        out_shape=jax.ShapeDtypeStruct((bsz, s, D_CONV), BF16),
        scratch_shapes=[pltpu.VMEM((D_CONV // LANES, s + 2 * CONV_HALO, LANES), F32),
                        pltpu.VMEM((D_CONV // LANES, s, LANES), F32)],
        compiler_params=pltpu.CompilerParams(dimension_semantics=("arbitrary",),
                                             vmem_limit_bytes=32 * MIB),
        name="conv",
    )(u, w_dw, b_dw.reshape(1, D_CONV), g_ln.reshape(1, D_CONV), b_ln.reshape(1, D_CONV))


def _att_kernel(consts_ref, q_ref, k_ref, v_ref, o_ref,
                bias_ref, bias3_ref, nat_ref, p4_ref, qp_ref, kp_ref, va_ref,
                p_ref, acc_ref, l_ref, *row_max_scratch, bounded):
    hp = pl.program_id(0)
    s = q_ref.shape[1]
    n_blk = s // Q_BLK
    d2, d3 = DILATIONS[1], DILATIONS[2]
    lane = lax.broadcasted_iota(jnp.int32, (1, LANES), 1)
    even = lane < HEAD_DIM

    @pl.when(pl.program_id(1) == 0)
    def _init_tables():
        def table(width, off, dil):
            row = lax.broadcasted_iota(jnp.int32, (2 * Q_BLK, width), 0)
            col = lax.broadcasted_iota(jnp.int32, (2 * Q_BLK, width), 1)
            slope = jnp.where(row < Q_BLK, consts_ref[2 * hp], consts_ref[2 * hp + 1])
            dist = jnp.abs(col - (row & (Q_BLK - 1)) - off)
            alibi = (-LOG2E * slope) * (dil * dist).astype(F32) - consts_ref[N_ATT_HEADS]
            return jnp.where(dist <= RADIUS, alibi, NEG_INF)

        for var, off in enumerate((0, RADIUS, 2 * RADIUS)):
            for p, dil in enumerate(DILATIONS[:2]):
                bias_ref[p, var] = table(K_WIN, off, dil)
        bias3_ref[...] = table(Q_BLK, 0, d3)
        for p in range(3):
            va_ref[p, :, LANES:2 * LANES] = jnp.ones((s, LANES), BF16)

    n4 = s // d2

    def stage(src_ref, put4, put16):
        for t0 in range(0, s, Q_BLK):
            nat_ref[t0:t0 + Q_BLK, :] = src_ref[0, t0:t0 + Q_BLK, :].astype(F32)
        for t0 in range(0, s, Q_BLK):
            r, u0 = t0 // n4, t0 % n4
            rows = nat_ref[pl.ds(r + d2 * u0, Q_BLK, stride=d2), :]
            p4_ref[t0:t0 + Q_BLK, :] = rows
            put4(t0, rows.astype(BF16))
        for r16 in range(d3):
            rows = p4_ref[pl.ds((r16 % d2) * n4 + r16 // d2, Q_BLK, stride=d3 // d2), :]
            put16(r16 * Q_BLK, rows.astype(BF16))

    def put_ref(ref, p):
        def put(t0, val):
            ref[p, t0:t0 + val.shape[0], 0:LANES] = val
        return put

    stage(q_ref, put_ref(qp_ref, 0), put_ref(qp_ref, 1))
    stage(k_ref, put_ref(kp_ref, 0), put_ref(kp_ref, 1))
    va_ref[0, :, 0:LANES] = v_ref[0]
    stage(v_ref, put_ref(va_ref, 1), put_ref(va_ref, 2))

    sub2 = n_blk // d2
    widths = (K_WIN, K_WIN, Q_BLK)

    def place(item):
        p, i = item
        rows = pl.ds(i * Q_BLK, Q_BLK)

        def window(j, n_sub):
            return (min(max(j * Q_BLK - RADIUS, 0), (n_sub - 2) * Q_BLK),
                    0 if j == 0 else (2 if j == n_sub - 1 else 1))

        if p == 0:
            ws, var = window(i, n_blk)
            return rows, pl.ds(ws, K_WIN), bias_ref.at[0, var], rows
        if p == 1:
            r, j = i // sub2, i % sub2
            ws, var = window(j, sub2)
            return rows, pl.ds(r * (sub2 * Q_BLK) + ws, K_WIN), bias_ref.at[1, var], rows
        return rows, rows, bias3_ref, pl.ds((i % d2) * n4 + i // d2, Q_BLK, stride=d3 // d2)

    q_refs = (q_ref.at[0], qp_ref.at[0], qp_ref.at[1])
    k_refs = (k_ref.at[0], kp_ref.at[0], kp_ref.at[1])

    def scores(item):
        rows, win, bias, _ = place(item)
        qb = q_refs[item[0]][rows, :]
        zero = jnp.zeros_like(qb)
        lhs = jnp.concatenate([jnp.where(even, qb, zero), jnp.where(even, zero, qb)], axis=0)
        sc = lax.dot_general(lhs, k_refs[item[0]][win, :], (((1,), (1,)), ((), ())),
                             preferred_element_type=F32)
        return sc + bias[...]

    def values(item, slot):
        p = item[0]
        _, win, _, dst = place(item)
        pv = _dot(p_ref[slot, :, 0:widths[p]], va_ref[p, win, :])
        acc_ref[p, dst, :] = jnp.where(even, pv[:Q_BLK, :LANES], pv[Q_BLK:, :LANES])
        l_ref[p, dst, :] = jnp.where(even, pv[:Q_BLK, LANES:], pv[Q_BLK:, LANES:])

    items = [(0, i) for i in range(n_blk)]
    for i in range(n_blk):
        items += [(1, i), (2, d2 * (i % d2) + i // d2)]
    depth = p_ref.shape[0]
    if not bounded:
        s_ref, m_ref = row_max_scratch

    def combine(c):
        rows = pl.ds(c * Q_BLK, Q_BLK)
        nat = pl.ds(c // sub2 + d2 * Q_BLK * (c % sub2), Q_BLK, stride=d2)
        if bounded:
            num = acc_ref[0, nat, :] + acc_ref[1, rows, :] + acc_ref[2, rows, :]
            den = l_ref[0, nat, :] + l_ref[1, rows, :] + l_ref[2, rows, :]
        else:
            m1, m2, m3 = m_ref[0, nat, :], m_ref[1, rows, :], m_ref[2, rows, :]
            mx = jnp.maximum(jnp.maximum(m1, m2), m3)
            w1, w2, w3 = jnp.exp2(m1 - mx), jnp.exp2(m2 - mx), jnp.exp2(m3 - mx)
            num = w1 * acc_ref[0, nat, :] + w2 * acc_ref[1, rows, :] + w3 * acc_ref[2, rows, :]
            den = w1 * l_ref[0, nat, :] + w2 * l_ref[1, rows, :] + w3 * l_ref[2, rows, :]
        nat_ref[nat, :] = num / den

    def finish(t):
        done = t + 1 - n_blk
        if done > 0 and done % (2 * sub2) == 0:
            r = done // (2 * sub2) - 1
            for c in range(r * sub2, (r + 1) * sub2):
                combine(c)

    if bounded:
        def probs(item, slot):
            p_ref[slot, :, 0:widths[item[0]]] = jnp.exp2(scores(item)).astype(BF16)

        for t in range(depth):
            probs(items[t], t)
        for t, item in enumerate(items):
            values(item, t % depth)
            finish(t)
            if t + depth < len(items):
                probs(items[t + depth], t % depth)
    else:
        def raw(item, slot):
            s_ref[slot, :, 0:widths[item[0]]] = scores(item)

        def softmax(item, slot):
            p = item[0]
            sc = s_ref[slot, :, 0:widths[p]]
            m = jnp.max(sc, axis=-1, keepdims=True)
            p_ref[slot, :, 0:widths[p]] = jnp.exp2(sc - m).astype(BF16)
            m_ref[p, place(item)[3], :] = jnp.where(even, m[:Q_BLK], m[Q_BLK:])

        for t in range(depth):
            raw(items[t], t)
        for t in range(depth):
            softmax(items[t], t)
            raw(items[t + depth], t)
        for t, item in enumerate(items):
            values(item, t % depth)
            finish(t)
            if t + depth < len(items):
                softmax(items[t + depth], t % depth)
            if t + 2 * depth < len(items):
                raw(items[t + 2 * depth], t % depth)

    def emit(i, carry):
        rows = pl.ds(pl.multiple_of(i * Q_BLK, Q_BLK), Q_BLK)
        o_ref[0, rows, :] = nat_ref[rows, :].astype(BF16)
        return carry

    lax.fori_loop(0, n_blk, emit, 0)


def _att_call(consts, q, k, v, *, bounded):
    bsz, s, _ = q.shape
    blk = pl.BlockSpec((1, s, LANES), lambda h, b: (b, 0, h))
    seq3 = pltpu.VMEM((3, s, LANES), F32)
    depth = 6 if bounded else 3
    scratch = [pltpu.VMEM((2, 3, 2 * Q_BLK, K_WIN), F32),
               pltpu.VMEM((2 * Q_BLK, Q_BLK), F32),
               pltpu.VMEM((s, LANES), F32),
               pltpu.VMEM((s, LANES), F32),
               pltpu.VMEM((2, s, LANES), BF16),
               pltpu.VMEM((2, s, LANES), BF16),
               pltpu.VMEM((3, s, 2 * LANES), BF16),
               pltpu.VMEM((depth, 2 * Q_BLK, K_WIN), BF16),
               seq3, seq3]
    if not bounded:
        scratch += [pltpu.VMEM((depth, 2 * Q_BLK, K_WIN), F32), seq3]
    return pl.pallas_call(
        functools.partial(_att_kernel, bounded=bounded),
        grid=(D_ATT // LANES, bsz),
        in_specs=[pl.BlockSpec(memory_space=pltpu.SMEM), blk, blk, blk],
        out_specs=blk,
        out_shape=jax.ShapeDtypeStruct((bsz, s, D_ATT), BF16),
        scratch_shapes=scratch,
        compiler_params=pltpu.CompilerParams(dimension_semantics=("arbitrary", "arbitrary"),
                                             vmem_limit_bytes=48 * MIB),
        name="att_bounded" if bounded else "att_rowmax",
    )(consts, q, k, v)


def _att(q, k, v, g_q, g_k):
    s = q.shape[1]
    assert s // DILATIONS[2] == Q_BLK and DILATIONS[2] == DILATIONS[1] ** 2
    slopes = jnp.asarray(2.0 ** (-8.0 * np.arange(1, N_ATT_HEADS + 1) / N_ATT_HEADS), dtype=F32)
    bound = (1.02 * LOG2E * HEAD_DIM ** 0.5) * jnp.max(jnp.abs(g_q)) * jnp.max(jnp.abs(g_k))
    pad = jnp.zeros((N_ATT_HEADS - 1,), F32)

    def consts(shift):
        return jnp.concatenate([slopes, jnp.reshape(shift, (1,)).astype(F32), pad])

    return lax.cond(2.0 * bound < MAX_SOFTMAX_SPAN,
                    lambda: _att_call(consts(bound), q, k, v, bounded=True),
                    lambda: _att_call(consts(0.0), q, k, v, bounded=False))


def _ffn_kernel(x_ref, yc_ref, ya_ref, mod_ref, g_ref, wo_ref, wg_ref, wu_ref, wd_ref,
                o_ref, x1_ref, h_ref, acc_ref):
    gate_m = mod_ref[0, 2:3, :]
    shift = mod_ref[0, 3:4, :]
    scale = mod_ref[0, 4:5, :]
    gate_f = mod_ref[0, 5:6, :]
    mix = _dot(yc_ref[0], wo_ref[0:D_CONV, :]) + _dot(ya_ref[0], wo_ref[D_CONV:, :])
    x1 = x_ref[0] + gate_m * mix
    x1_ref[...] = x1
    r = lax.rsqrt(jnp.mean(x1 * x1, axis=-1, keepdims=True) + EPS)
    h_ref[...] = (((x1 * r) * g_ref[...]) * (1.0 + scale) + shift).astype(BF16)
    acc_ref[...] = jnp.zeros_like(acc_ref)
    for c0 in range(0, D_FF, FF_CHUNK):
        h = h_ref[...]
        g = _dot(h, wg_ref[:, c0:c0 + FF_CHUNK])
        u = _dot(h, wu_ref[:, c0:c0 + FF_CHUNK])
        a = ((g * jax.nn.sigmoid(g)) * u).astype(BF16)
        acc_ref[...] += _dot(a, wd_ref[c0:c0 + FF_CHUNK, :])
    o_ref[0] = x1_ref[...] + gate_f * acc_ref[...]


def _ffn(x, yc, ya, mod, g_ffn, w_out, w_gate, w_up, w_down):
    bsz, s, d = x.shape
    ts = TOKEN_TILE
    const = dict(pipeline_mode=pl.Buffered(1))
    row = lambda b, i: (b, i, 0)
    fix = lambda b, i: (0, 0)
    return pl.pallas_call(
        _ffn_kernel,
        grid=(bsz, s // ts),
        in_specs=[pl.BlockSpec((1, ts, d), row),
                  pl.BlockSpec((1, ts, D_CONV), row),
                  pl.BlockSpec((1, ts, D_ATT), row),
                  pl.BlockSpec((1, N_MOD, d), lambda b, i: (b, 0, 0)),
                  pl.BlockSpec((1, d), fix, **const),
                  pl.BlockSpec((d, d), fix, **const),
                  pl.BlockSpec((d, D_FF), fix, **const),
                  pl.BlockSpec((d, D_FF), fix, **const),
                  pl.BlockSpec((D_FF, d), fix, **const)],
        out_specs=pl.BlockSpec((1, ts, d), row),
        out_shape=jax.ShapeDtypeStruct((bsz, s, d), F32),
        scratch_shapes=[pltpu.VMEM((ts, d), F32), pltpu.VMEM((ts, d), BF16),
                        pltpu.VMEM((ts, d), F32)],
        compiler_params=pltpu.CompilerParams(dimension_semantics=("arbitrary", "arbitrary"),
                                             vmem_limit_bytes=56 * MIB),
        name="ffn",
    )(x, yc, ya, mod, g_ffn.reshape(1, d), w_out, w_gate, w_up, w_down)


def kernel(x, c, w_ada, b_ada, g_mix, w_in, w_dw, b_dw, g_conv_ln, b_conv_ln, g_q, g_k,
           w_out, g_ffn, w_gate, w_up, w_down):
    bsz, s, d = x.shape
    for l in range(w_ada.shape[0]):
        mod = _ada(c, w_ada[l], b_ada[l]).reshape(bsz, N_MOD, d)
        u, q, k, v = _inp(x, mod, g_mix[l], w_in[l].astype(BF16), g_q[l], g_k[l])
        yc = _conv(u, w_dw[l], b_dw[l], g_conv_ln[l], b_conv_ln[l])
        ya = _att(q, k, v, g_q[l], g_k[l])
        x = _ffn(x, yc, ya, mod, g_ffn[l], w_out[l].astype(BF16), w_gate[l].astype(BF16),
                 w_up[l].astype(BF16), w_down[l].astype(BF16))
    return x
```

```python
import functools

import jax
import jax.numpy as jnp
import numpy as np
from jax import lax
from jax.experimental import pallas as pl
from jax.experimental.pallas import tpu as pltpu

F32 = jnp.float32
BF16 = jnp.bfloat16

D_MODEL = 1024
D_CONV = 512
D_ATT = 512
N_ATT_HEADS = 8
HEAD_DIM = 64
CONV_WIDTH = 31
CONV_PAD = CONV_WIDTH // 2
DILATIONS = (1, 4, 16)
RADIUS = 64
D_IN = 2 * D_CONV + 3 * D_ATT
D_FF = 2816
N_MOD = 6
EPS = 1e-6
NEG_INF = -1e30
LOG2E = 1.4426950408889634

LANES = 128
MXU_DIM = 256
MIB = 1024 * 1024
TOKEN_TILE = 512
Q_BLK = 128
K_WIN = Q_BLK + 2 * RADIUS
FF_CHUNK = MXU_DIM
CONV_HALO = 16
CONV_ROWS = 64
MAX_SOFTMAX_SPAN = 120.0


def _dot(a, b):
    return jnp.dot(a, b, preferred_element_type=F32)


def _split_bf16(a):
    hi = a.astype(BF16)
    lo = (a - hi.astype(F32)).astype(BF16)
    return hi, lo


def _ada_kernel(c_ref, w_ref, b_ref, o_ref):
    c = c_ref[...]
    a_hi, a_lo = _split_bf16(c * jax.nn.sigmoid(c))
    w_hi, w_lo = _split_bf16(w_ref[...])
    o_ref[...] = _dot(a_hi, w_hi) + _dot(a_hi, w_lo) + _dot(a_lo, w_hi) + b_ref[...]


def _ada(c, w, b):
    bsz, d = c.shape
    n = w.shape[1]
    tn = 1024
    return pl.pallas_call(
        _ada_kernel,
        grid=(n // tn,),
        in_specs=[pl.BlockSpec((bsz, d), lambda j: (0, 0)),
                  pl.BlockSpec((d, tn), lambda j: (0, j)),
                  pl.BlockSpec((1, tn), lambda j: (0, j))],
        out_specs=pl.BlockSpec((bsz, tn), lambda j: (0, j)),
        out_shape=jax.ShapeDtypeStruct((bsz, n), F32),
        compiler_params=pltpu.CompilerParams(dimension_semantics=("arbitrary",),
                                             vmem_limit_bytes=32 * MIB),
        name="ada",
    )(c, w, b.reshape(1, n))


def _head_norm(t, ones, gain):
    sq = (t * t).astype(BF16)
    ssq = jnp.concatenate([_dot(sq[:, c:c + MXU_DIM], ones) for c in range(0, D_ATT, MXU_DIM)],
                          axis=1)
    return t * lax.rsqrt(ssq * (1.0 / HEAD_DIM) + EPS) * gain


def _inp_kernel(x_ref, mod_ref, g_ref, w_ref, ones_ref, gq_ref, gk_ref,
                u_ref, q_ref, k_ref, v_ref):
    x = x_ref[0]
    shift = mod_ref[0, 0:1, :]
    scale = mod_ref[0, 1:2, :]
    r = lax.rsqrt(jnp.mean(x * x, axis=-1, keepdims=True) + EPS)
    h = ((x * r) * g_ref[...]) * (1.0 + scale) + shift
    hb = h.astype(BF16)
    a = _dot(hb, w_ref[:, 0:D_CONV])
    g = _dot(hb, w_ref[:, D_CONV:2 * D_CONV])
    u_ref[0] = (a * jax.nn.sigmoid(g)).astype(BF16)
    ones = ones_ref[...]
    o = 2 * D_CONV
    q = _dot(hb, w_ref[:, o:o + D_ATT])
    q_ref[0] = (_head_norm(q, ones, gq_ref[...]) * (HEAD_DIM ** -0.5 * LOG2E)).astype(BF16)
    k = _dot(hb, w_ref[:, o + D_ATT:o + 2 * D_ATT])
    k_ref[0] = _head_norm(k, ones, gk_ref[...]).astype(BF16)
    v_ref[0] = _dot(hb, w_ref[:, o + 2 * D_ATT:o + 3 * D_ATT]).astype(BF16)


def _inp(x, mod, g_mix, w_in, g_q, g_k):
    bsz, s, d = x.shape
    ts = TOKEN_TILE
    head = np.arange(MXU_DIM) // HEAD_DIM
    ones = jnp.asarray(head[:, None] == head[None, :], dtype=BF16)
    gq = jnp.tile(g_q, N_ATT_HEADS).reshape(1, D_ATT)
    gk = jnp.tile(g_k, N_ATT_HEADS).reshape(1, D_ATT)
    const = dict(pipeline_mode=pl.Buffered(1))
    row = lambda b, i: (b, i, 0)
    fix = lambda b, i: (0, 0)
    out = jax.ShapeDtypeStruct((bsz, s, D_ATT), BF16)
    return pl.pallas_call(
        _inp_kernel,
        grid=(bsz, s // ts),
        in_specs=[pl.BlockSpec((1, ts, d), row),
                  pl.BlockSpec((1, N_MOD, d), lambda b, i: (b, 0, 0)),
                  pl.BlockSpec((1, d), fix, **const),
                  pl.BlockSpec((d, D_IN), fix, **const),
                  pl.BlockSpec((MXU_DIM, MXU_DIM), fix, **const),
                  pl.BlockSpec((1, D_ATT), fix, **const),
                  pl.BlockSpec((1, D_ATT), fix, **const)],
        out_specs=[pl.BlockSpec((1, ts, D_ATT), row)] * 4,
        out_shape=[out] * 4,
        compiler_params=pltpu.CompilerParams(dimension_semantics=("arbitrary", "arbitrary"),
                                             vmem_limit_bytes=48 * MIB),
        name="inp",
    )(x, mod, g_mix.reshape(1, d), w_in, ones, gq, gk)


def _conv_kernel(u_ref, w_ref, b_ref, g_ref, beta_ref, y_ref, upad_ref, cv_ref):
    s = u_ref.shape[1]
    rows = CONV_ROWS
    n_cb = D_CONV // LANES
    zeros = jnp.zeros((CONV_HALO, LANES), F32)
    for c in range(n_cb):
        upad_ref[c, 0:CONV_HALO, :] = zeros
        upad_ref[c, CONV_HALO + s:CONV_HALO + s + CONV_HALO, :] = zeros

    def fill(i, carry):
        t0 = pl.multiple_of(i * rows, rows)
        uf = u_ref[0, pl.ds(t0, rows), :].astype(F32)
        for c in range(n_cb):
            upad_ref[c, pl.ds(CONV_HALO + t0, rows), :] = uf[:, c * LANES:(c + 1) * LANES]
        return carry

    lax.fori_loop(0, s // rows, fill, 0)

    def taps(i, carry):
        t0 = pl.multiple_of(i * rows, rows)
        for c in range(n_cb):
            cs = slice(c * LANES, (c + 1) * LANES)
            acc = jnp.broadcast_to(b_ref[:, cs], (rows, LANES))
            for k in range(CONV_WIDTH):
                tap = upad_ref[c, pl.ds(t0 + (CONV_HALO - CONV_PAD) + k, rows), :]
                acc = acc + w_ref[k:k + 1, cs] * tap
            cv_ref[c, pl.ds(t0, rows), :] = acc
        return carry

    lax.fori_loop(0, s // rows, taps, 0)

    def norm(i, carry):
        t0 = pl.multiple_of(i * rows, rows)
        conv = [cv_ref[c, pl.ds(t0, rows), :] for c in range(n_cb)]
        tot = conv[0] + conv[1] + conv[2] + conv[3]
        mu = jnp.sum(tot, axis=-1, keepdims=True) * (1.0 / D_CONV)
        cen = [t - mu for t in conv]
        sq = cen[0] * cen[0] + cen[1] * cen[1] + cen[2] * cen[2] + cen[3] * cen[3]
        var = jnp.sum(sq, axis=-1, keepdims=True) * (1.0 / D_CONV)
        rstd = lax.rsqrt(var + EPS)
        for c in range(n_cb):
            cs = slice(c * LANES, (c + 1) * LANES)
            z = (cen[c] * rstd) * g_ref[:, cs] + beta_ref[:, cs]
            y_ref[0, pl.ds(t0, rows), cs] = (z * jax.nn.sigmoid(z)).astype(BF16)
        return carry

    lax.fori_loop(0, s // rows, norm, 0, unroll=4)


def _conv(u, w_dw, b_dw, g_ln, b_ln):
    bsz, s, _ = u.shape
    fix = lambda b: (0, 0)
    vec = pl.BlockSpec((1, D_CONV), fix)
    return pl.pallas_call(
        _conv_kernel,
        grid=(bsz,),
        in_specs=[pl.BlockSpec((1, s, D_CONV), lambda b: (b, 0, 0)),
                  pl.BlockSpec((CONV_WIDTH, D_CONV), fix), vec, vec, vec],
        out_specs=pl.BlockSpec((1, s, D_CONV), lambda b: (b, 0, 0)),
        out_shape=jax.ShapeDtypeStruct((bsz, s, D_CONV), BF16),
        scratch_shapes=[pltpu.VMEM((D_CONV // LANES, s + 2 * CONV_HALO, LANES), F32),
                        pltpu.VMEM((D_CONV // LANES, s, LANES), F32)],
        compiler_params=pltpu.CompilerParams(dimension_semantics=("arbitrary",),
                                             vmem_limit_bytes=32 * MIB),
        name="conv",
    )(u, w_dw, b_dw.reshape(1, D_CONV), g_ln.reshape(1, D_CONV), b_ln.reshape(1, D_CONV))


def _att_kernel(consts_ref, q_ref, k_ref, v_ref, o_ref,
                bias_ref, bias3_ref, nat_ref, p4_ref, qp_ref, kp_ref, va_ref,
                p_ref, acc_ref, l_ref, *row_max_scratch, bounded):
    hp = pl.program_id(0)
    s = q_ref.shape[1]
    n_blk = s // Q_BLK
    d2, d3 = DILATIONS[1], DILATIONS[2]
    lane = lax.broadcasted_iota(jnp.int32, (1, LANES), 1)
    even = lane < HEAD_DIM

    @pl.when(pl.program_id(1) == 0)
    def _init_tables():
        def table(width, off, dil):
            row = lax.broadcasted_iota(jnp.int32, (2 * Q_BLK, width), 0)
            col = lax.broadcasted_iota(jnp.int32, (2 * Q_BLK, width), 1)
            slope = jnp.where(row < Q_BLK, consts_ref[2 * hp], consts_ref[2 * hp + 1])
            dist = jnp.abs(col - (row & (Q_BLK - 1)) - off)
            alibi = (-LOG2E * slope) * (dil * dist).astype(F32) - consts_ref[N_ATT_HEADS]
            return jnp.where(dist <= RADIUS, alibi, NEG_INF)

        for var, off in enumerate((0, RADIUS, 2 * RADIUS)):
            for p, dil in enumerate(DILATIONS[:2]):
                bias_ref[p, var] = table(K_WIN, off, dil)
        bias3_ref[...] = table(Q_BLK, 0, d3)
        for p in range(3):
            va_ref[p, :, LANES:2 * LANES] = jnp.ones((s, LANES), BF16)

    n4 = s // d2

    def stage(src_ref, put4, put16):
        for t0 in range(0, s, Q_BLK):
            nat_ref[t0:t0 + Q_BLK, :] = src_ref[0, t0:t0 + Q_BLK, :].astype(F32)
        for t0 in range(0, s, Q_BLK):
            r, u0 = t0 // n4, t0 % n4
            rows = nat_ref[pl.ds(r + d2 * u0, Q_BLK, stride=d2), :]
            p4_ref[t0:t0 + Q_BLK, :] = rows
            put4(t0, rows.astype(BF16))
        for r16 in range(d3):
            rows = p4_ref[pl.ds((r16 % d2) * n4 + r16 // d2, Q_BLK, stride=d3 // d2), :]
            put16(r16 * Q_BLK, rows.astype(BF16))

    def put_ref(ref, p):
        def put(t0, val):
            ref[p, t0:t0 + val.shape[0], 0:LANES] = val
        return put

    stage(q_ref, put_ref(qp_ref, 0), put_ref(qp_ref, 1))
    stage(k_ref, put_ref(kp_ref, 0), put_ref(kp_ref, 1))
    va_ref[0, :, 0:LANES] = v_ref[0]
    stage(v_ref, put_ref(va_ref, 1), put_ref(va_ref, 2))

    sub2 = n_blk // d2
    widths = (K_WIN, K_WIN, Q_BLK)

    def place(item):
        p, i = item
        rows = pl.ds(i * Q_BLK, Q_BLK)

        def window(j, n_sub):
            return (min(max(j * Q_BLK - RADIUS, 0), (n_sub - 2) * Q_BLK),
                    0 if j == 0 else (2 if j == n_sub - 1 else 1))

        if p == 0:
            ws, var = window(i, n_blk)
            return rows, pl.ds(ws, K_WIN), bias_ref.at[0, var], rows
        if p == 1:
            r, j = i // sub2, i % sub2
            ws, var = window(j, sub2)
            return rows, pl.ds(r * (sub2 * Q_BLK) + ws, K_WIN), bias_ref.at[1, var], rows
        return rows, rows, bias3_ref, pl.ds((i % d2) * n4 + i // d2, Q_BLK, stride=d3 // d2)

    q_refs = (q_ref.at[0], qp_ref.at[0], qp_ref.at[1])
    k_refs = (k_ref.at[0], kp_ref.at[0], kp_ref.at[1])

    def scores(item):
        rows, win, bias, _ = place(item)
        qb = q_refs[item[0]][rows, :]
        zero = jnp.zeros_like(qb)
        lhs = jnp.concatenate([jnp.where(even, qb, zero), jnp.where(even, zero, qb)], axis=0)
        sc = lax.dot_general(lhs, k_refs[item[0]][win, :], (((1,), (1,)), ((), ())),
                             preferred_element_type=F32)
        return sc + bias[...]

    def values(item, slot):
        p = item[0]
        _, win, _, dst = place(item)
        pv = _dot(p_ref[slot, :, 0:widths[p]], va_ref[p, win, :])
        acc_ref[p, dst, :] = jnp.where(even, pv[:Q_BLK, :LANES], pv[Q_BLK:, :LANES])
        l_ref[p, dst, :] = jnp.where(even, pv[:Q_BLK, LANES:], pv[Q_BLK:, LANES:])

    items = [(0, i) for i in range(n_blk)]
    for i in range(n_blk):
        items += [(1, i), (2, d2 * (i % d2) + i // d2)]
    depth = p_ref.shape[0]
    if not bounded:
        s_ref, m_ref = row_max_scratch

    def combine(c):
        rows = pl.ds(c * Q_BLK, Q_BLK)
        nat = pl.ds(c // sub2 + d2 * Q_BLK * (c % sub2), Q_BLK, stride=d2)
        if bounded:
            num = acc_ref[0, nat, :] + acc_ref[1, rows, :] + acc_ref[2, rows, :]
            den = l_ref[0, nat, :] + l_ref[1, rows, :] + l_ref[2, rows, :]
        else:
            m1, m2, m3 = m_ref[0, nat, :], m_ref[1, rows, :], m_ref[2, rows, :]
            mx = jnp.maximum(jnp.maximum(m1, m2), m3)
            w1, w2, w3 = jnp.exp2(m1 - mx), jnp.exp2(m2 - mx), jnp.exp2(m3 - mx)
            num = w1 * acc_ref[0, nat, :] + w2 * acc_ref[1, rows, :] + w3 * acc_ref[2, rows, :]
            den = w1 * l_ref[0, nat, :] + w2 * l_ref[1, rows, :] + w3 * l_ref[2, rows, :]
        nat_ref[nat, :] = num / den

    def finish(t):
        done = t + 1 - n_blk
        if done > 0 and done % (2 * sub2) == 0:
            r = done // (2 * sub2) - 1
            for c in range(r * sub2, (r + 1) * sub2):
                combine(c)

    if bounded:
        def probs(item, slot):
            p_ref[slot, :, 0:widths[item[0]]] = jnp.exp2(scores(item)).astype(BF16)

        for t in range(depth):
            probs(items[t], t)
        for t, item in enumerate(items):
            values(item, t % depth)
            finish(t)
            if t + depth < len(items):
                probs(items[t + depth], t % depth)
    else:
        def raw(item, slot):
            s_ref[slot, :, 0:widths[item[0]]] = scores(item)

        def softmax(item, slot):
            p = item[0]
            sc = s_ref[slot, :, 0:widths[p]]
            m = jnp.max(sc, axis=-1, keepdims=True)
            p_ref[slot, :, 0:widths[p]] = jnp.exp2(sc - m).astype(BF16)
            m_ref[p, place(item)[3], :] = jnp.where(even, m[:Q_BLK], m[Q_BLK:])

        for t in range(depth):
            raw(items[t], t)
        for t in range(depth):
            softmax(items[t], t)
            raw(items[t + depth], t)
        for t, item in enumerate(items):
            values(item, t % depth)
            finish(t)
            if t + depth < len(items):
                softmax(items[t + depth], t % depth)
            if t + 2 * depth < len(items):
                raw(items[t + 2 * depth], t % depth)

    def emit(i, carry):
        rows = pl.ds(pl.multiple_of(i * Q_BLK, Q_BLK), Q_BLK)
        o_ref[0, rows, :] = nat_ref[rows, :].astype(BF16)
        return carry

    lax.fori_loop(0, n_blk, emit, 0)


def _att_call(consts, q, k, v, *, bounded):
    bsz, s, _ = q.shape
    blk = pl.BlockSpec((1, s, LANES), lambda h, b: (b, 0, h))
    seq3 = pltpu.VMEM((3, s, LANES), F32)
    depth = 6 if bounded else 3
    scratch = [pltpu.VMEM((2, 3, 2 * Q_BLK, K_WIN), F32),
               pltpu.VMEM((2 * Q_BLK, Q_BLK), F32),
               pltpu.VMEM((s, LANES), F32),
               pltpu.VMEM((s, LANES), F32),
               pltpu.VMEM((2, s, LANES), BF16),
               pltpu.VMEM((2, s, LANES), BF16),
               pltpu.VMEM((3, s, 2 * LANES), BF16),
               pltpu.VMEM((depth, 2 * Q_BLK, K_WIN), BF16),
               seq3, seq3]
    if not bounded:
        scratch += [pltpu.VMEM((depth, 2 * Q_BLK, K_WIN), F32), seq3]
    return pl.pallas_call(
        functools.partial(_att_kernel, bounded=bounded),
        grid=(D_ATT // LANES, bsz),
        in_specs=[pl.BlockSpec(memory_space=pltpu.SMEM), blk, blk, blk],
        out_specs=blk,
        out_shape=jax.ShapeDtypeStruct((bsz, s, D_ATT), BF16),
        scratch_shapes=scratch,
        compiler_params=pltpu.CompilerParams(dimension_semantics=("arbitrary", "arbitrary"),
                                             vmem_limit_bytes=48 * MIB),
        name="att_bounded" if bounded else "att_rowmax",
    )(consts, q, k, v)


def _att(q, k, v, g_q, g_k):
    s = q.shape[1]
    assert s // DILATIONS[2] == Q_BLK and DILATIONS[2] == DILATIONS[1] ** 2
    slopes = jnp.asarray(2.0 ** (-8.0 * np.arange(1, N_ATT_HEADS + 1) / N_ATT_HEADS), dtype=F32)
    bound = (1.02 * LOG2E * HEAD_DIM ** 0.5) * jnp.max(jnp.abs(g_q)) * jnp.max(jnp.abs(g_k))
    pad = jnp.zeros((N_ATT_HEADS - 1,), F32)

    def consts(shift):
        return jnp.concatenate([slopes, jnp.reshape(shift, (1,)).astype(F32), pad])

    return lax.cond(2.0 * bound < MAX_SOFTMAX_SPAN,
                    lambda: _att_call(consts(bound), q, k, v, bounded=True),
                    lambda: _att_call(consts(0.0), q, k, v, bounded=False))


def _ffn_kernel(x_ref, yc_ref, ya_ref, mod_ref, g_ref, wo_ref, wg_ref, wu_ref, wd_ref,
                o_ref, x1_ref, h_ref, acc_ref):
    gate_m = mod_ref[0, 2:3, :]
    shift = mod_ref[0, 3:4, :]
    scale = mod_ref[0, 4:5, :]
    gate_f = mod_ref[0, 5:6, :]
    mix = _dot(yc_ref[0], wo_ref[0:D_CONV, :]) + _dot(ya_ref[0], wo_ref[D_CONV:, :])
    x1 = x_ref[0] + gate_m * mix
    x1_ref[...] = x1
    r = lax.rsqrt(jnp.mean(x1 * x1, axis=-1, keepdims=True) + EPS)
    h_ref[...] = (((x1 * r) * g_ref[...]) * (1.0 + scale) + shift).astype(BF16)
    acc_ref[...] = jnp.zeros_like(acc_ref)
    for c0 in range(0, D_FF, FF_CHUNK):
        h = h_ref[...]
        g = _dot(h, wg_ref[:, c0:c0 + FF_CHUNK])
        u = _dot(h, wu_ref[:, c0:c0 + FF_CHUNK])
        a = ((g * jax.nn.sigmoid(g)) * u).astype(BF16)
        acc_ref[...] += _dot(a, wd_ref[c0:c0 + FF_CHUNK, :])
    o_ref[0] = x1_ref[...] + gate_f * acc_ref[...]


def _ffn(x, yc, ya, mod, g_ffn, w_out, w_gate, w_up, w_down):
    bsz, s, d = x.shape
    ts = TOKEN_TILE
    const = dict(pipeline_mode=pl.Buffered(1))
    row = lambda b, i: (b, i, 0)
    fix = lambda b, i: (0, 0)
    return pl.pallas_call(
        _ffn_kernel,
        grid=(bsz, s // ts),
        in_specs=[pl.BlockSpec((1, ts, d), row),
                  pl.BlockSpec((1, ts, D_CONV), row),
                  pl.BlockSpec((1, ts, D_ATT), row),
                  pl.BlockSpec((1, N_MOD, d), lambda b, i: (b, 0, 0)),
                  pl.BlockSpec((1, d), fix, **const),
                  pl.BlockSpec((d, d), fix, **const),
                  pl.BlockSpec((d, D_FF), fix, **const),
                  pl.BlockSpec((d, D_FF), fix, **const),
                  pl.BlockSpec((D_FF, d), fix, **const)],
        out_specs=pl.BlockSpec((1, ts, d), row),
        out_shape=jax.ShapeDtypeStruct((bsz, s, d), F32),
        scratch_shapes=[pltpu.VMEM((ts, d), F32), pltpu.VMEM((ts, d), BF16),
                        pltpu.VMEM((ts, d), F32)],
        compiler_params=pltpu.CompilerParams(dimension_semantics=("arbitrary", "arbitrary"),
                                             vmem_limit_bytes=56 * MIB),
        name="ffn",
    )(x, yc, ya, mod, g_ffn.reshape(1, d), w_out, w_gate, w_up, w_down)


_CONV_LANES = [slice(c * LANES, (c + 1) * LANES) for c in range(D_CONV // LANES)]
CONV_GROUP = 16


def _conv_stage(up_ref, u_ref, un_ref, upad_ref, has_prev, has_next):
    ts = u_ref.shape[1]
    for c, cs in enumerate(_CONV_LANES):
        upad_ref[c, 0:CONV_HALO, :] = jnp.where(has_prev, up_ref[0, :, cs].astype(F32), 0.0)
        upad_ref[c, CONV_HALO:CONV_HALO + ts, :] = u_ref[0, :, cs].astype(F32)
        upad_ref[c, CONV_HALO + ts:, :] = jnp.where(has_next, un_ref[0, :, cs].astype(F32), 0.0)


def _conv_rows(t0, zero, w_ref, b_ref, g_ref, beta_ref, upad_ref, put):
    conv = []
    for c, cs in enumerate(_CONV_LANES):
        acc = jnp.broadcast_to(b_ref[:, cs], (CONV_GROUP, LANES)) + jnp.tile(zero, (CONV_GROUP // 8, 1))
        for k in range(CONV_WIDTH):
            lo = t0 + CONV_HALO - CONV_PAD + k
            acc = acc + w_ref[k:k + 1, cs] * upad_ref[c, lo:lo + CONV_GROUP, :]
        conv.append(acc)
    tot = conv[0] + conv[1] + conv[2] + conv[3]
    mu = jnp.sum(tot, axis=-1, keepdims=True) * (1.0 / D_CONV)
    cen = [t - mu for t in conv]
    sq = cen[0] * cen[0] + cen[1] * cen[1] + cen[2] * cen[2] + cen[3] * cen[3]
    rstd = lax.rsqrt(jnp.sum(sq, axis=-1, keepdims=True) * (1.0 / D_CONV) + EPS)
    out = []
    for c, cs in enumerate(_CONV_LANES):
        z = (cen[c] * rstd) * g_ref[:, cs] + beta_ref[:, cs]
        out.append(z * jax.nn.sigmoid(z))
        put(t0, cs, out[-1].astype(BF16))
    return (out[0] + out[1] + out[2] + out[3])[0:8, :]


def _mix_ffn_kernel(x_ref, ya_ref, mod_ref, up_ref, u_ref, un_ref, cw_ref, cb_ref, cg_ref,
                    cbeta_ref, zero_ref, g_ref, wo_ref, wg_ref, wu_ref, wd_ref, o_ref,
                    yc_ref, upad_ref, x1_ref, h_ref, acc_ref, *, tiles_per_seq):
    n = pl.program_id(0)
    last = pl.num_programs(0) - 2

    @pl.when(n == 0)
    def _no_previous_tile():
        yc_ref[1] = jnp.zeros(yc_ref.shape[1:], BF16)

    gate_m = mod_ref[0, 2:3, :]
    shift = mod_ref[0, 3:4, :]
    scale = mod_ref[0, 4:5, :]
    gate_f = mod_ref[0, 5:6, :]
    mix = _dot(yc_ref[(n + 1) % 2], wo_ref[0:D_CONV, :]) + _dot(ya_ref[0], wo_ref[D_CONV:, :])
    x1 = x_ref[0] + gate_m * mix
    x1_ref[...] = x1
    r = lax.rsqrt(jnp.mean(x1 * x1, axis=-1, keepdims=True) + EPS)
    h_ref[...] = (((x1 * r) * g_ref[...]) * (1.0 + scale) + shift).astype(BF16)

    tile = jnp.minimum(n, last) % tiles_per_seq

    def put(t0, cs, rows):
        yc_ref[n % 2, t0:t0 + rows.shape[0], cs] = rows

    def exact_zero(v):
        return pltpu.bitcast(pltpu.bitcast(v, jnp.int32) & zero_ref[...], F32)

    _conv_stage(up_ref, u_ref, un_ref, upad_ref, tile > 0, tile < tiles_per_seq - 1)

    chunks = list(range(0, D_FF, FF_CHUNK))
    groups = list(range(0, x_ref.shape[1], CONV_GROUP))
    hosts = len(chunks) - 1
    share = [groups[(j * len(groups)) // hosts:((j + 1) * len(groups)) // hosts]
             for j in range(hosts)] + [[]]
    acc_ref[...] = jnp.zeros_like(acc_ref)
    for j, c0 in enumerate(chunks):
        h = h_ref[...]
        g = _dot(h, wg_ref[:, c0:c0 + FF_CHUNK])
        u = _dot(h, wu_ref[:, c0:c0 + FF_CHUNK])
        acc_ref[...] += _dot(((g * jax.nn.sigmoid(g)) * u).astype(BF16), wd_ref[c0:c0 + FF_CHUNK, :])
        if share[j]:
            start = exact_zero(acc_ref[8:16, 0:LANES])
            for t0 in share[j]:
                tok = _conv_rows(t0, start, cw_ref, cb_ref, cg_ref, cbeta_ref, upad_ref, put)
                acc_ref[0:8, 0:LANES] += exact_zero(tok)
    o_ref[0] = x1_ref[...] + gate_f * acc_ref[...]


def _mix_ffn(x, u, ya, mod, w_dw, b_dw, g_ln, b_ln, g_ffn, w_out, w_gate, w_up, w_down):
    bsz, s, d = x.shape
    ts = TOKEN_TILE
    nt = s // ts
    total = bsz * nt
    hb = ts // CONV_HALO

    def done(n):
        f = jnp.maximum(n - 1, 0)
        return f // nt, f % nt

    def conv(n):
        c = jnp.minimum(n, total - 1)
        return c // nt, c % nt

    const = dict(pipeline_mode=pl.Buffered(1))
    fix = lambda n: (0, 0)
    vec = pl.BlockSpec((1, D_CONV), fix, **const)
    halo = lambda edge: pl.BlockSpec((1, CONV_HALO, D_CONV), edge)
    return pl.pallas_call(
        functools.partial(_mix_ffn_kernel, tiles_per_seq=nt),
        grid=(total + 1,),
        in_specs=[pl.BlockSpec((1, ts, d), lambda n: (*done(n), 0)),
                  pl.BlockSpec((1, ts, D_ATT), lambda n: (*done(n), 0)),
                  pl.BlockSpec((1, N_MOD, d), lambda n: (done(n)[0], 0, 0)),
                  halo(lambda n: (conv(n)[0], jnp.maximum(conv(n)[1] * hb - 1, 0), 0)),
                  pl.BlockSpec((1, ts, D_CONV), lambda n: (*conv(n), 0)),
                  halo(lambda n: (conv(n)[0], jnp.minimum((conv(n)[1] + 1) * hb, nt * hb - 1), 0)),
                  pl.BlockSpec((CONV_WIDTH, D_CONV), fix, **const), vec, vec, vec,
                  pl.BlockSpec((8, LANES), fix, **const),
                  pl.BlockSpec((1, d), fix, **const),
                  pl.BlockSpec((d, d), fix, **const),
                  pl.BlockSpec((d, D_FF), fix, **const),
                  pl.BlockSpec((d, D_FF), fix, **const),
                  pl.BlockSpec((D_FF, d), fix, **const)],
        out_specs=pl.BlockSpec((1, ts, d), lambda n: (*done(n), 0)),
        out_shape=jax.ShapeDtypeStruct((bsz, s, d), F32),
        scratch_shapes=[pltpu.VMEM((2, ts, D_CONV), BF16),
                        pltpu.VMEM((D_CONV // LANES, ts + 2 * CONV_HALO, LANES), F32),
                        pltpu.VMEM((ts, d), F32), pltpu.VMEM((ts, d), BF16),
                        pltpu.VMEM((ts, d), F32)],
        compiler_params=pltpu.CompilerParams(dimension_semantics=("arbitrary",),
                                             vmem_limit_bytes=56 * MIB),
        name="mix_ffn",
    )(x, ya, mod, u, u, u, w_dw, b_dw.reshape(1, D_CONV), g_ln.reshape(1, D_CONV),
      b_ln.reshape(1, D_CONV), jnp.zeros((8, LANES), jnp.int32), g_ffn.reshape(1, d),
      w_out, w_gate, w_up, w_down)


def kernel(x, c, w_ada, b_ada, g_mix, w_in, w_dw, b_dw, g_conv_ln, b_conv_ln, g_q, g_k,
           w_out, g_ffn, w_gate, w_up, w_down):
    bsz, s, d = x.shape
    for l in range(w_ada.shape[0]):
        mod = _ada(c, w_ada[l], b_ada[l]).reshape(bsz, N_MOD, d)
        u, q, k, v = _inp(x, mod, g_mix[l], w_in[l].astype(BF16), g_q[l], g_k[l])
        ya = _att(q, k, v, g_q[l], g_k[l])
        x = _mix_ffn(x, u, ya, mod, w_dw[l], b_dw[l], g_conv_ln[l], b_conv_ln[l], g_ffn[l],
                     w_out[l].astype(BF16), w_gate[l].astype(BF16), w_up[l].astype(BF16),
                     w_down[l].astype(BF16))
    return x
```

```python
import functools

import jax
import jax.numpy as jnp
import numpy as np
from jax import lax
from jax.experimental import pallas as pl
from jax.experimental.pallas import tpu as pltpu

F32 = jnp.float32
BF16 = jnp.bfloat16

D_MODEL = 1024
D_CONV = 512
D_ATT = 512
N_ATT_HEADS = 8
HEAD_DIM = 64
CONV_WIDTH = 31
CONV_PAD = CONV_WIDTH // 2
DILATIONS = (1, 4, 16)
RADIUS = 64
D_IN = 2 * D_CONV + 3 * D_ATT
D_FF = 2816
N_MOD = 6
EPS = 1e-6
NEG_INF = -1e30
LOG2E = 1.4426950408889634

LANES = 128
MXU_DIM = 256
MIB = 1024 * 1024
TOKEN_TILE = 512
INP_TILES = 2
Q_BLK = 128
K_WIN = Q_BLK + 2 * RADIUS
FF_CHUNK = MXU_DIM
CONV_HALO = 16
CONV_GROUP = 32
MAX_SOFTMAX_SPAN = 120.0


def _dot(a, b):
    return jnp.dot(a, b, preferred_element_type=F32)


def _split_bf16(a):
    hi = a.astype(BF16)
    lo = (a - hi.astype(F32)).astype(BF16)
    return hi, lo


def _ada_kernel(c_ref, w_ref, b_ref, o_ref):
    c = c_ref[...]
    a_hi, a_lo = _split_bf16(c * jax.nn.sigmoid(c))
    w_hi, w_lo = _split_bf16(w_ref[...])
    o_ref[...] = _dot(a_hi, w_hi) + _dot(a_hi, w_lo) + _dot(a_lo, w_hi) + b_ref[...]


def _ada(c, w, b):
    bsz, d = c.shape
    n = w.shape[1]
    tn = 1024
    return pl.pallas_call(
        _ada_kernel,
        grid=(n // tn,),
        in_specs=[pl.BlockSpec((bsz, d), lambda j: (0, 0)),
                  pl.BlockSpec((d, tn), lambda j: (0, j)),
                  pl.BlockSpec((1, tn), lambda j: (0, j))],
        out_specs=pl.BlockSpec((bsz, tn), lambda j: (0, j)),
        out_shape=jax.ShapeDtypeStruct((bsz, n), F32),
        compiler_params=pltpu.CompilerParams(dimension_semantics=("arbitrary",),
                                             vmem_limit_bytes=32 * MIB),
        name="ada",
    )(c, w, b.reshape(1, n))


def _head_norm(t, ones, gain):
    sq = (t * t).astype(BF16)
    ssq = jnp.concatenate([_dot(sq[:, c:c + MXU_DIM], ones) for c in range(0, D_ATT, MXU_DIM)],
                          axis=1)
    return t * lax.rsqrt(ssq * (1.0 / HEAD_DIM) + EPS) * gain


def _inp_kernel(x_ref, mod_ref, g_ref, w_ref, ones_ref, gq_ref, gk_ref,
                u_ref, q_ref, k_ref, v_ref):
    shift = mod_ref[0, 0:1, :]
    scale = mod_ref[0, 1:2, :]
    ones = ones_ref[...]
    o = 2 * D_CONV
    for t0 in range(0, x_ref.shape[1], TOKEN_TILE):
        rows = slice(t0, t0 + TOKEN_TILE)
        x = x_ref[0, rows, :]
        r = lax.rsqrt(jnp.mean(x * x, axis=-1, keepdims=True) + EPS)
        h = ((x * r) * g_ref[...]) * (1.0 + scale) + shift
        hb = h.astype(BF16)
        a = _dot(hb, w_ref[:, 0:D_CONV])
        g = _dot(hb, w_ref[:, D_CONV:2 * D_CONV])
        u_ref[0, rows, :] = (a * jax.nn.sigmoid(g)).astype(BF16)
        q = _dot(hb, w_ref[:, o:o + D_ATT])
        q_ref[0, rows, :] = (_head_norm(q, ones, gq_ref[...])
                             * (HEAD_DIM ** -0.5 * LOG2E)).astype(BF16)
        k = _dot(hb, w_ref[:, o + D_ATT:o + 2 * D_ATT])
        k_ref[0, rows, :] = _head_norm(k, ones, gk_ref[...]).astype(BF16)
        v_ref[0, rows, :] = _dot(hb, w_ref[:, o + 2 * D_ATT:o + 3 * D_ATT]).astype(BF16)


def _inp(x, mod, g_mix, w_in, g_q, g_k):
    bsz, s, d = x.shape
    ts = INP_TILES * TOKEN_TILE
    head = np.arange(MXU_DIM) // HEAD_DIM
    ones = jnp.asarray(head[:, None] == head[None, :], dtype=BF16)
    gq = jnp.tile(g_q, N_ATT_HEADS).reshape(1, D_ATT)
    gk = jnp.tile(g_k, N_ATT_HEADS).reshape(1, D_ATT)
    const = dict(pipeline_mode=pl.Buffered(1))
    row = lambda b, i: (b, i, 0)
    fix = lambda b, i: (0, 0)
    out = jax.ShapeDtypeStruct((bsz, s, D_ATT), BF16)
    return pl.pallas_call(
        _inp_kernel,
        grid=(bsz, s // ts),
        in_specs=[pl.BlockSpec((1, ts, d), row),
                  pl.BlockSpec((1, N_MOD, d), lambda b, i: (b, 0, 0)),
                  pl.BlockSpec((1, d), fix, **const),
                  pl.BlockSpec((d, D_IN), fix, **const),
                  pl.BlockSpec((MXU_DIM, MXU_DIM), fix, **const),
                  pl.BlockSpec((1, D_ATT), fix, **const),
                  pl.BlockSpec((1, D_ATT), fix, **const)],
        out_specs=[pl.BlockSpec((1, ts, D_ATT), row)] * 4,
        out_shape=[out] * 4,
        compiler_params=pltpu.CompilerParams(dimension_semantics=("arbitrary", "arbitrary"),
                                             vmem_limit_bytes=48 * MIB),
        name="inp",
    )(x, mod, g_mix.reshape(1, d), w_in, ones, gq, gk)


def _att_kernel(consts_ref, q_ref, k_ref, v_ref, o_ref,
                bias_ref, bias3_ref, nat_ref, p4_ref, qp_ref, kp_ref, va_ref,
                p_ref, acc_ref, l_ref, *row_max_scratch, bounded):
    hp = pl.program_id(0)
    s = q_ref.shape[1]
    n_blk = s // Q_BLK
    d2, d3 = DILATIONS[1], DILATIONS[2]
    lane = lax.broadcasted_iota(jnp.int32, (1, LANES), 1)
    even = lane < HEAD_DIM

    @pl.when(pl.program_id(1) == 0)
    def _init_tables():
        def table(width, off, dil):
            row = lax.broadcasted_iota(jnp.int32, (2 * Q_BLK, width), 0)
            col = lax.broadcasted_iota(jnp.int32, (2 * Q_BLK, width), 1)
            slope = jnp.where(row < Q_BLK, consts_ref[2 * hp], consts_ref[2 * hp + 1])
            dist = jnp.abs(col - (row & (Q_BLK - 1)) - off)
            alibi = (-LOG2E * slope) * (dil * dist).astype(F32) - consts_ref[N_ATT_HEADS]
            return jnp.where(dist <= RADIUS, alibi, NEG_INF)

        for var, off in enumerate((0, RADIUS, 2 * RADIUS)):
            for p, dil in enumerate(DILATIONS[:2]):
                bias_ref[p, var] = table(K_WIN, off, dil)
        bias3_ref[...] = table(Q_BLK, 0, d3)
        for p in range(3):
            va_ref[p, :, LANES:2 * LANES] = jnp.ones((s, LANES), BF16)

    n4 = s // d2

    def stage(src_ref, put4, put16):
        for t0 in range(0, s, Q_BLK):
            nat_ref[t0:t0 + Q_BLK, :] = src_ref[0, t0:t0 + Q_BLK, :].astype(F32)
        for t0 in range(0, s, Q_BLK):
            r, u0 = t0 // n4, t0 % n4
            rows = nat_ref[pl.ds(r + d2 * u0, Q_BLK, stride=d2), :]
            p4_ref[t0:t0 + Q_BLK, :] = rows
            put4(t0, rows.astype(BF16))
        for r16 in range(d3):
            rows = p4_ref[pl.ds((r16 % d2) * n4 + r16 // d2, Q_BLK, stride=d3 // d2), :]
            put16(r16 * Q_BLK, rows.astype(BF16))

    def put_ref(ref, p):
        def put(t0, val):
            ref[p, t0:t0 + val.shape[0], 0:LANES] = val
        return put

    stage(q_ref, put_ref(qp_ref, 0), put_ref(qp_ref, 1))
    stage(k_ref, put_ref(kp_ref, 0), put_ref(kp_ref, 1))
    va_ref[0, :, 0:LANES] = v_ref[0]
    stage(v_ref, put_ref(va_ref, 1), put_ref(va_ref, 2))

    sub2 = n_blk // d2
    widths = (K_WIN, K_WIN, Q_BLK)

    def place(item):
        p, i = item
        rows = pl.ds(i * Q_BLK, Q_BLK)

        def window(j, n_sub):
            return (min(max(j * Q_BLK - RADIUS, 0), (n_sub - 2) * Q_BLK),
                    0 if j == 0 else (2 if j == n_sub - 1 else 1))

        if p == 0:
            ws, var = window(i, n_blk)
            return rows, pl.ds(ws, K_WIN), bias_ref.at[0, var], rows
        if p == 1:
            r, j = i // sub2, i % sub2
            ws, var = window(j, sub2)
            return rows, pl.ds(r * (sub2 * Q_BLK) + ws, K_WIN), bias_ref.at[1, var], rows
        return rows, rows, bias3_ref, pl.ds((i % d2) * n4 + i // d2, Q_BLK, stride=d3 // d2)

    q_refs = (q_ref.at[0], qp_ref.at[0], qp_ref.at[1])
    k_refs = (k_ref.at[0], kp_ref.at[0], kp_ref.at[1])

    def scores(item):
        rows, win, bias, _ = place(item)
        qb = q_refs[item[0]][rows, :]
        zero = jnp.zeros_like(qb)
        lhs = jnp.concatenate([jnp.where(even, qb, zero), jnp.where(even, zero, qb)], axis=0)
        sc = lax.dot_general(lhs, k_refs[item[0]][win, :], (((1,), (1,)), ((), ())),
                             preferred_element_type=F32)
        return sc + bias[...]

    def values(item, slot):
        p = item[0]
        _, win, _, dst = place(item)
        pv = _dot(p_ref[slot, :, 0:widths[p]], va_ref[p, win, :])
        acc_ref[p, dst, :] = jnp.where(even, pv[:Q_BLK, :LANES], pv[Q_BLK:, :LANES])
        l_ref[p, dst, :] = jnp.where(even, pv[:Q_BLK, LANES:], pv[Q_BLK:, LANES:])

    items = [(0, i) for i in range(n_blk)]
    for i in range(n_blk):
        items += [(1, i), (2, d2 * (i % d2) + i // d2)]
    depth = p_ref.shape[0]
    if not bounded:
        s_ref, m_ref = row_max_scratch

    def combine(c):
        rows = pl.ds(c * Q_BLK, Q_BLK)
        nat = pl.ds(c // sub2 + d2 * Q_BLK * (c % sub2), Q_BLK, stride=d2)
        if bounded:
            num = acc_ref[0, nat, :] + acc_ref[1, rows, :] + acc_ref[2, rows, :]
            den = l_ref[0, nat, :] + l_ref[1, rows, :] + l_ref[2, rows, :]
        else:
            m1, m2, m3 = m_ref[0, nat, :], m_ref[1, rows, :], m_ref[2, rows, :]
            mx = jnp.maximum(jnp.maximum(m1, m2), m3)
            w1, w2, w3 = jnp.exp2(m1 - mx), jnp.exp2(m2 - mx), jnp.exp2(m3 - mx)
            num = w1 * acc_ref[0, nat, :] + w2 * acc_ref[1, rows, :] + w3 * acc_ref[2, rows, :]
            den = w1 * l_ref[0, nat, :] + w2 * l_ref[1, rows, :] + w3 * l_ref[2, rows, :]
        nat_ref[nat, :] = num / den

    def finish(t):
        done = t + 1 - n_blk
        if done > 0 and done % (2 * sub2) == 0:
            r = done // (2 * sub2) - 1
            for c in range(r * sub2, (r + 1) * sub2):
                combine(c)

    if bounded:
        def probs(item, slot):
            p_ref[slot, :, 0:widths[item[0]]] = jnp.exp2(scores(item)).astype(BF16)

        for t in range(depth):
            probs(items[t], t)
        for t, item in enumerate(items):
            values(item, t % depth)
            finish(t)
            if t + depth < len(items):
                probs(items[t + depth], t % depth)
    else:
        def raw(item, slot):
            s_ref[slot, :, 0:widths[item[0]]] = scores(item)

        def softmax(item, slot):
            p = item[0]
            sc = s_ref[slot, :, 0:widths[p]]
            m = jnp.max(sc, axis=-1, keepdims=True)
            p_ref[slot, :, 0:widths[p]] = jnp.exp2(sc - m).astype(BF16)
            m_ref[p, place(item)[3], :] = jnp.where(even, m[:Q_BLK], m[Q_BLK:])

        for t in range(depth):
            raw(items[t], t)
        for t in range(depth):
            softmax(items[t], t)
            raw(items[t + depth], t)
        for t, item in enumerate(items):
            values(item, t % depth)
            finish(t)
            if t + depth < len(items):
                softmax(items[t + depth], t % depth)
            if t + 2 * depth < len(items):
                raw(items[t + 2 * depth], t % depth)

    def emit(i, carry):
        rows = pl.ds(pl.multiple_of(i * Q_BLK, Q_BLK), Q_BLK)
        o_ref[0, rows, :] = nat_ref[rows, :].astype(BF16)
        return carry

    lax.fori_loop(0, n_blk, emit, 0)


def _att_call(consts, q, k, v, *, bounded):
    bsz, s, _ = q.shape
    blk = pl.BlockSpec((1, s, LANES), lambda h, b: (b, 0, h))
    seq3 = pltpu.VMEM((3, s, LANES), F32)
    depth = 6 if bounded else 3
    scratch = [pltpu.VMEM((2, 3, 2 * Q_BLK, K_WIN), F32),
               pltpu.VMEM((2 * Q_BLK, Q_BLK), F32),
               pltpu.VMEM((s, LANES), F32),
               pltpu.VMEM((s, LANES), F32),
               pltpu.VMEM((2, s, LANES), BF16),
               pltpu.VMEM((2, s, LANES), BF16),
               pltpu.VMEM((3, s, 2 * LANES), BF16),
               pltpu.VMEM((depth, 2 * Q_BLK, K_WIN), BF16),
               seq3, seq3]
    if not bounded:
        scratch += [pltpu.VMEM((depth, 2 * Q_BLK, K_WIN), F32), seq3]
    return pl.pallas_call(
        functools.partial(_att_kernel, bounded=bounded),
        grid=(D_ATT // LANES, bsz),
        in_specs=[pl.BlockSpec(memory_space=pltpu.SMEM), blk, blk, blk],
        out_specs=blk,
        out_shape=jax.ShapeDtypeStruct((bsz, s, D_ATT), BF16),
        scratch_shapes=scratch,
        compiler_params=pltpu.CompilerParams(dimension_semantics=("arbitrary", "arbitrary"),
                                             vmem_limit_bytes=48 * MIB),
        name="att_bounded" if bounded else "att_rowmax",
    )(consts, q, k, v)


def _att(q, k, v, g_q, g_k):
    s = q.shape[1]
    assert s // DILATIONS[2] == Q_BLK and DILATIONS[2] == DILATIONS[1] ** 2
    slopes = jnp.asarray(2.0 ** (-8.0 * np.arange(1, N_ATT_HEADS + 1) / N_ATT_HEADS), dtype=F32)
    bound = (1.02 * LOG2E * HEAD_DIM ** 0.5) * jnp.max(jnp.abs(g_q)) * jnp.max(jnp.abs(g_k))
    pad = jnp.zeros((N_ATT_HEADS - 1,), F32)

    def consts(shift):
        return jnp.concatenate([slopes, jnp.reshape(shift, (1,)).astype(F32), pad])

    return lax.cond(2.0 * bound < MAX_SOFTMAX_SPAN,
                    lambda: _att_call(consts(bound), q, k, v, bounded=True),
                    lambda: _att_call(consts(0.0), q, k, v, bounded=False))


_CONV_LANES = [slice(c * LANES, (c + 1) * LANES) for c in range(D_CONV // LANES)]


def _conv_stage(up_ref, u_ref, un_ref, upad_ref, has_prev, has_next):
    ts = u_ref.shape[1]
    for c, cs in enumerate(_CONV_LANES):
        upad_ref[c, 0:CONV_HALO, :] = jnp.where(has_prev, up_ref[0, :, cs].astype(F32), 0.0)
        upad_ref[c, CONV_HALO:CONV_HALO + ts, :] = u_ref[0, :, cs].astype(F32)
        upad_ref[c, CONV_HALO + ts:, :] = jnp.where(has_next, un_ref[0, :, cs].astype(F32), 0.0)


def _conv_rows(t0, zero, w_ref, b_ref, g_ref, beta_ref, upad_ref, put):
    conv = []
    for c, cs in enumerate(_CONV_LANES):
        acc = jnp.broadcast_to(b_ref[:, cs], (CONV_GROUP, LANES)) + jnp.tile(zero, (CONV_GROUP // 8, 1))
        for k in range(CONV_WIDTH):
            lo = t0 + CONV_HALO - CONV_PAD + k
            acc = acc + w_ref[k:k + 1, cs] * upad_ref[c, lo:lo + CONV_GROUP, :]
        conv.append(acc)
    tot = conv[0] + conv[1] + conv[2] + conv[3]
    mu = jnp.sum(tot, axis=-1, keepdims=True) * (1.0 / D_CONV)
    cen = [t - mu for t in conv]
    sq = cen[0] * cen[0] + cen[1] * cen[1] + cen[2] * cen[2] + cen[3] * cen[3]
    rstd = lax.rsqrt(jnp.sum(sq, axis=-1, keepdims=True) * (1.0 / D_CONV) + EPS)
    out = []
    for c, cs in enumerate(_CONV_LANES):
        z = (cen[c] * rstd) * g_ref[:, cs] + beta_ref[:, cs]
        out.append(z * jax.nn.sigmoid(z))
        put(t0, cs, out[-1].astype(BF16))
    return (out[0] + out[1] + out[2] + out[3])[0:8, :]


def _mix_ffn_kernel(x_ref, ya_ref, mod_ref, up_ref, u_ref, un_ref, cw_ref, cb_ref, cg_ref,
                    cbeta_ref, zero_ref, g_ref, wo_ref, wg_ref, wu_ref, wd_ref, o_ref,
                    yc_ref, upad_ref, x1_ref, h_ref, acc_ref, *, tiles_per_seq):
    n = pl.program_id(0)
    last = pl.num_programs(0) - 2

    @pl.when(n == 0)
    def _no_previous_tile():
        yc_ref[1] = jnp.zeros(yc_ref.shape[1:], BF16)

    gate_m = mod_ref[0, 2:3, :]
    shift = mod_ref[0, 3:4, :]
    scale = mod_ref[0, 4:5, :]
    gate_f = mod_ref[0, 5:6, :]
    mix = _dot(yc_ref[(n + 1) % 2], wo_ref[0:D_CONV, :]) + _dot(ya_ref[0], wo_ref[D_CONV:, :])
    x1 = x_ref[0] + gate_m * mix
    x1_ref[...] = x1
    r = lax.rsqrt(jnp.mean(x1 * x1, axis=-1, keepdims=True) + EPS)
    h_ref[...] = (((x1 * r) * g_ref[...]) * (1.0 + scale) + shift).astype(BF16)

    tile = jnp.minimum(n, last) % tiles_per_seq

    def put(t0, cs, rows):
        yc_ref[n % 2, t0:t0 + rows.shape[0], cs] = rows

    def exact_zero(v):
        return pltpu.bitcast(pltpu.bitcast(v, jnp.int32) & zero_ref[...], F32)

    _conv_stage(up_ref, u_ref, un_ref, upad_ref, tile > 0, tile < tiles_per_seq - 1)

    chunks = list(range(0, D_FF, FF_CHUNK))
    groups = list(range(0, x_ref.shape[1], CONV_GROUP))
    hosts = len(chunks) - 1
    share = [groups[(j * len(groups)) // hosts:((j + 1) * len(groups)) // hosts]
             for j in range(hosts)] + [[]]
    acc_ref[...] = jnp.zeros_like(acc_ref)
    for j, c0 in enumerate(chunks):
        h = h_ref[...]
        g = _dot(h, wg_ref[:, c0:c0 + FF_CHUNK])
        u = _dot(h, wu_ref[:, c0:c0 + FF_CHUNK])
        acc_ref[...] += _dot(((g * jax.nn.sigmoid(g)) * u).astype(BF16), wd_ref[c0:c0 + FF_CHUNK, :])
        if share[j]:
            start = exact_zero(acc_ref[8:16, 0:LANES])
            for t0 in share[j]:
                tok = _conv_rows(t0, start, cw_ref, cb_ref, cg_ref, cbeta_ref, upad_ref, put)
                acc_ref[0:8, 0:LANES] += exact_zero(tok)
    o_ref[0] = x1_ref[...] + gate_f * acc_ref[...]


def _mix_ffn(x, u, ya, mod, w_dw, b_dw, g_ln, b_ln, g_ffn, w_out, w_gate, w_up, w_down):
    bsz, s, d = x.shape
    ts = TOKEN_TILE
    nt = s // ts
    total = bsz * nt
    hb = ts // CONV_HALO

    def done(n):
        f = jnp.maximum(n - 1, 0)
        return f // nt, f % nt

    def conv(n):
        c = jnp.minimum(n, total - 1)
        return c // nt, c % nt

    const = dict(pipeline_mode=pl.Buffered(1))
    fix = lambda n: (0, 0)
    vec = pl.BlockSpec((1, D_CONV), fix, **const)
    halo = lambda edge: pl.BlockSpec((1, CONV_HALO, D_CONV), edge)
    return pl.pallas_call(
        functools.partial(_mix_ffn_kernel, tiles_per_seq=nt),
        grid=(total + 1,),
        in_specs=[pl.BlockSpec((1, ts, d), lambda n: (*done(n), 0)),
                  pl.BlockSpec((1, ts, D_ATT), lambda n: (*done(n), 0)),
                  pl.BlockSpec((1, N_MOD, d), lambda n: (done(n)[0], 0, 0)),
                  halo(lambda n: (conv(n)[0], jnp.maximum(conv(n)[1] * hb - 1, 0), 0)),
                  pl.BlockSpec((1, ts, D_CONV), lambda n: (*conv(n), 0)),
                  halo(lambda n: (conv(n)[0], jnp.minimum((conv(n)[1] + 1) * hb, nt * hb - 1), 0)),
                  pl.BlockSpec((CONV_WIDTH, D_CONV), fix, **const), vec, vec, vec,
                  pl.BlockSpec((8, LANES), fix, **const),
                  pl.BlockSpec((1, d), fix, **const),
                  pl.BlockSpec((d, d), fix, **const),
                  pl.BlockSpec((d, D_FF), fix, **const),
                  pl.BlockSpec((d, D_FF), fix, **const),
                  pl.BlockSpec((D_FF, d), fix, **const)],
        out_specs=pl.BlockSpec((1, ts, d), lambda n: (*done(n), 0)),
        out_shape=jax.ShapeDtypeStruct((bsz, s, d), F32),
        scratch_shapes=[pltpu.VMEM((2, ts, D_CONV), BF16),
                        pltpu.VMEM((D_CONV // LANES, ts + 2 * CONV_HALO, LANES), F32),
                        pltpu.VMEM((ts, d), F32), pltpu.VMEM((ts, d), BF16),
                        pltpu.VMEM((ts, d), F32)],
        compiler_params=pltpu.CompilerParams(dimension_semantics=("arbitrary",),
                                             vmem_limit_bytes=56 * MIB),
        name="mix_ffn",
    )(x, ya, mod, u, u, u, w_dw, b_dw.reshape(1, D_CONV), g_ln.reshape(1, D_CONV),
      b_ln.reshape(1, D_CONV), jnp.zeros((8, LANES), jnp.int32), g_ffn.reshape(1, d),
      w_out, w_gate, w_up, w_down)


def kernel(x, c, w_ada, b_ada, g_mix, w_in, w_dw, b_dw, g_conv_ln, b_conv_ln, g_q, g_k,
           w_out, g_ffn, w_gate, w_up, w_down):
    bsz, s, d = x.shape
    for l in range(w_ada.shape[0]):
        mod = _ada(c, w_ada[l], b_ada[l]).reshape(bsz, N_MOD, d)
        u, q, k, v = _inp(x, mod, g_mix[l], w_in[l].astype(BF16), g_q[l], g_k[l])
        ya = _att(q, k, v, g_q[l], g_k[l])
        x = _mix_ffn(x, u, ya, mod, w_dw[l], b_dw[l], g_conv_ln[l], b_conv_ln[l], g_ffn[l],
                     w_out[l].astype(BF16), w_gate[l].astype(BF16), w_up[l].astype(BF16),
                     w_down[l].astype(BF16))
    return x
```

```python
import functools

import jax
import jax.numpy as jnp
import numpy as np
from jax import lax
from jax.experimental import pallas as pl
from jax.experimental.pallas import tpu as pltpu

F32 = jnp.float32
BF16 = jnp.bfloat16

D_MODEL = 1024
D_CONV = 512
D_ATT = 512
N_ATT_HEADS = 8
HEAD_DIM = 64
CONV_WIDTH = 31
CONV_PAD = CONV_WIDTH // 2
DILATIONS = (1, 4, 16)
RADIUS = 64
D_IN = 2 * D_CONV + 3 * D_ATT
D_FF = 2816
N_MOD = 6
EPS = 1e-6
NEG_INF = -1e30
LOG2E = 1.4426950408889634

LANES = 128
MXU_DIM = 256
MIB = 1024 * 1024
TOKEN_TILE = 512
INP_TILES = 4
Q_BLK = 128
K_WIN = Q_BLK + 2 * RADIUS
FF_CHUNK = MXU_DIM
CONV_HALO = 16
CONV_GROUP = 32
MAX_SOFTMAX_SPAN = 120.0


def _dot(a, b):
    return jnp.dot(a, b, preferred_element_type=F32)


def _split_bf16(a):
    hi = a.astype(BF16)
    lo = (a - hi.astype(F32)).astype(BF16)
    return hi, lo


def _ada_kernel(c_ref, w_ref, b_ref, o_ref):
    c = c_ref[...]
    a_hi, a_lo = _split_bf16(c * jax.nn.sigmoid(c))
    w_hi, w_lo = _split_bf16(w_ref[...])
    o_ref[...] = _dot(a_hi, w_hi) + _dot(a_hi, w_lo) + _dot(a_lo, w_hi) + b_ref[...]


def _ada(c, w, b):
    bsz, d = c.shape
    n = w.shape[1]
    tn = 1024
    return pl.pallas_call(
        _ada_kernel,
        grid=(n // tn,),
        in_specs=[pl.BlockSpec((bsz, d), lambda j: (0, 0)),
                  pl.BlockSpec((d, tn), lambda j: (0, j)),
                  pl.BlockSpec((1, tn), lambda j: (0, j))],
        out_specs=pl.BlockSpec((bsz, tn), lambda j: (0, j)),
        out_shape=jax.ShapeDtypeStruct((bsz, n), F32),
        compiler_params=pltpu.CompilerParams(dimension_semantics=("arbitrary",),
                                             vmem_limit_bytes=32 * MIB),
        name="ada",
    )(c, w, b.reshape(1, n))


def _head_norm(t, ones, gain):
    sq = (t * t).astype(BF16)
    ssq = jnp.concatenate([_dot(sq[:, c:c + MXU_DIM], ones) for c in range(0, D_ATT, MXU_DIM)],
                          axis=1)
    return t * lax.rsqrt(ssq * (1.0 / HEAD_DIM) + EPS) * gain


def _inp_kernel(x_ref, mod_ref, g_ref, w_ref, ones_ref, gq_ref, gk_ref,
                u_ref, q_ref, k_ref, v_ref):
    shift = mod_ref[0, 0:1, :]
    scale = mod_ref[0, 1:2, :]
    ones = ones_ref[...]
    o = 2 * D_CONV
    for t0 in range(0, x_ref.shape[1], TOKEN_TILE):
        rows = slice(t0, t0 + TOKEN_TILE)
        x = x_ref[0, rows, :]
        r = lax.rsqrt(jnp.mean(x * x, axis=-1, keepdims=True) + EPS)
        h = ((x * r) * g_ref[...]) * (1.0 + scale) + shift
        hb = h.astype(BF16)
        a = _dot(hb, w_ref[:, 0:D_CONV])
        g = _dot(hb, w_ref[:, D_CONV:2 * D_CONV])
        u_ref[0, rows, :] = (a * jax.nn.sigmoid(g)).astype(BF16)
        q = _dot(hb, w_ref[:, o:o + D_ATT])
        q_ref[0, rows, :] = (_head_norm(q, ones, gq_ref[...])
                             * (HEAD_DIM ** -0.5 * LOG2E)).astype(BF16)
        k = _dot(hb, w_ref[:, o + D_ATT:o + 2 * D_ATT])
        k_ref[0, rows, :] = _head_norm(k, ones, gk_ref[...]).astype(BF16)
        v_ref[0, rows, :] = _dot(hb, w_ref[:, o + 2 * D_ATT:o + 3 * D_ATT]).astype(BF16)


def _inp(x, mod, g_mix, w_in, g_q, g_k):
    bsz, s, d = x.shape
    ts = INP_TILES * TOKEN_TILE
    head = np.arange(MXU_DIM) // HEAD_DIM
    ones = jnp.asarray(head[:, None] == head[None, :], dtype=BF16)
    gq = jnp.tile(g_q, N_ATT_HEADS).reshape(1, D_ATT)
    gk = jnp.tile(g_k, N_ATT_HEADS).reshape(1, D_ATT)
    const = dict(pipeline_mode=pl.Buffered(1))
    row = lambda b, i: (b, i, 0)
    fix = lambda b, i: (0, 0)
    out = jax.ShapeDtypeStruct((bsz, s, D_ATT), BF16)
    return pl.pallas_call(
        _inp_kernel,
        grid=(bsz, s // ts),
        in_specs=[pl.BlockSpec((1, ts, d), row),
                  pl.BlockSpec((1, N_MOD, d), lambda b, i: (b, 0, 0)),
                  pl.BlockSpec((1, d), fix, **const),
                  pl.BlockSpec((d, D_IN), fix, **const),
                  pl.BlockSpec((MXU_DIM, MXU_DIM), fix, **const),
                  pl.BlockSpec((1, D_ATT), fix, **const),
                  pl.BlockSpec((1, D_ATT), fix, **const)],
        out_specs=[pl.BlockSpec((1, ts, D_ATT), row)] * 4,
        out_shape=[out] * 4,
        compiler_params=pltpu.CompilerParams(dimension_semantics=("arbitrary", "arbitrary"),
                                             vmem_limit_bytes=56 * MIB),
        name="inp",
    )(x, mod, g_mix.reshape(1, d), w_in, ones, gq, gk)


def _att_kernel(consts_ref, q_ref, k_ref, v_ref, o_ref,
                bias_ref, bias3_ref, nat_ref, p4_ref, qp_ref, kp_ref, va_ref,
                p_ref, acc_ref, l_ref, *row_max_scratch, bounded):
    hp = pl.program_id(0)
    s = q_ref.shape[1]
    n_blk = s // Q_BLK
    d2, d3 = DILATIONS[1], DILATIONS[2]
    lane = lax.broadcasted_iota(jnp.int32, (1, LANES), 1)
    even = lane < HEAD_DIM

    @pl.when(pl.program_id(1) == 0)
    def _init_tables():
        def table(width, off, dil):
            row = lax.broadcasted_iota(jnp.int32, (2 * Q_BLK, width), 0)
            col = lax.broadcasted_iota(jnp.int32, (2 * Q_BLK, width), 1)
            slope = jnp.where(row < Q_BLK, consts_ref[2 * hp], consts_ref[2 * hp + 1])
            dist = jnp.abs(col - (row & (Q_BLK - 1)) - off)
            alibi = (-LOG2E * slope) * (dil * dist).astype(F32) - consts_ref[N_ATT_HEADS]
            return jnp.where(dist <= RADIUS, alibi, NEG_INF)

        for var, off in enumerate((0, RADIUS, 2 * RADIUS)):
            for p, dil in enumerate(DILATIONS[:2]):
                bias_ref[p, var] = table(K_WIN, off, dil)
        bias3_ref[...] = table(Q_BLK, 0, d3)
        for p in range(3):
            va_ref[p, :, LANES:2 * LANES] = jnp.ones((s, LANES), BF16)

    n4 = s // d2

    def stage(src_ref, put4, put16):
        for t0 in range(0, s, Q_BLK):
            nat_ref[t0:t0 + Q_BLK, :] = src_ref[0, t0:t0 + Q_BLK, :].astype(F32)
        for t0 in range(0, s, Q_BLK):
            r, u0 = t0 // n4, t0 % n4
            rows = nat_ref[pl.ds(r + d2 * u0, Q_BLK, stride=d2), :]
            p4_ref[t0:t0 + Q_BLK, :] = rows
            put4(t0, rows.astype(BF16))
        for r16 in range(d3):
            rows = p4_ref[pl.ds((r16 % d2) * n4 + r16 // d2, Q_BLK, stride=d3 // d2), :]
            put16(r16 * Q_BLK, rows.astype(BF16))

    def put_ref(ref, p):
        def put(t0, val):
            ref[p, t0:t0 + val.shape[0], 0:LANES] = val
        return put

    stage(q_ref, put_ref(qp_ref, 0), put_ref(qp_ref, 1))
    stage(k_ref, put_ref(kp_ref, 0), put_ref(kp_ref, 1))
    va_ref[0, :, 0:LANES] = v_ref[0]
    stage(v_ref, put_ref(va_ref, 1), put_ref(va_ref, 2))

    sub2 = n_blk // d2
    widths = (K_WIN, K_WIN, Q_BLK)

    def place(item):
        p, i = item
        rows = pl.ds(i * Q_BLK, Q_BLK)

        def window(j, n_sub):
            return (min(max(j * Q_BLK - RADIUS, 0), (n_sub - 2) * Q_BLK),
                    0 if j == 0 else (2 if j == n_sub - 1 else 1))

        if p == 0:
            ws, var = window(i, n_blk)
            return rows, pl.ds(ws, K_WIN), bias_ref.at[0, var], rows
        if p == 1:
            r, j = i // sub2, i % sub2
            ws, var = window(j, sub2)
            return rows, pl.ds(r * (sub2 * Q_BLK) + ws, K_WIN), bias_ref.at[1, var], rows
        return rows, rows, bias3_ref, pl.ds((i % d2) * n4 + i // d2, Q_BLK, stride=d3 // d2)

    q_refs = (q_ref.at[0], qp_ref.at[0], qp_ref.at[1])
    k_refs = (k_ref.at[0], kp_ref.at[0], kp_ref.at[1])

    def scores(item):
        rows, win, bias, _ = place(item)
        qb = q_refs[item[0]][rows, :]
        zero = jnp.zeros_like(qb)
        lhs = jnp.concatenate([jnp.where(even, qb, zero), jnp.where(even, zero, qb)], axis=0)
        sc = lax.dot_general(lhs, k_refs[item[0]][win, :], (((1,), (1,)), ((), ())),
                             preferred_element_type=F32)
        return sc + bias[...]

    def values(item, slot):
        p = item[0]
        _, win, _, dst = place(item)
        pv = _dot(p_ref[slot, :, 0:widths[p]], va_ref[p, win, :])
        acc_ref[p, dst, :] = jnp.where(even, pv[:Q_BLK, :LANES], pv[Q_BLK:, :LANES])
        l_ref[p, dst, :] = jnp.where(even, pv[:Q_BLK, LANES:], pv[Q_BLK:, LANES:])

    items = [(0, i) for i in range(n_blk)]
    for i in range(n_blk):
        items += [(1, i), (2, d2 * (i % d2) + i // d2)]
    depth = p_ref.shape[0]
    if not bounded:
        s_ref, m_ref = row_max_scratch

    def combine(c):
        rows = pl.ds(c * Q_BLK, Q_BLK)
        nat = pl.ds(c // sub2 + d2 * Q_BLK * (c % sub2), Q_BLK, stride=d2)
        if bounded:
            num = acc_ref[0, nat, :] + acc_ref[1, rows, :] + acc_ref[2, rows, :]
            den = l_ref[0, nat, :] + l_ref[1, rows, :] + l_ref[2, rows, :]
        else:
            m1, m2, m3 = m_ref[0, nat, :], m_ref[1, rows, :], m_ref[2, rows, :]
            mx = jnp.maximum(jnp.maximum(m1, m2), m3)
            w1, w2, w3 = jnp.exp2(m1 - mx), jnp.exp2(m2 - mx), jnp.exp2(m3 - mx)
            num = w1 * acc_ref[0, nat, :] + w2 * acc_ref[1, rows, :] + w3 * acc_ref[2, rows, :]
            den = w1 * l_ref[0, nat, :] + w2 * l_ref[1, rows, :] + w3 * l_ref[2, rows, :]
        nat_ref[nat, :] = num / den

    def finish(t):
        done = t + 1 - n_blk
        if done > 0 and done % (2 * sub2) == 0:
            r = done // (2 * sub2) - 1
            for c in range(r * sub2, (r + 1) * sub2):
                combine(c)

    if bounded:
        def probs(item, slot):
            p_ref[slot, :, 0:widths[item[0]]] = jnp.exp2(scores(item)).astype(BF16)

        for t in range(depth):
            probs(items[t], t)
        for t, item in enumerate(items):
            values(item, t % depth)
            finish(t)
            if t + depth < len(items):
                probs(items[t + depth], t % depth)
    else:
        def raw(item, slot):
            s_ref[slot, :, 0:widths[item[0]]] = scores(item)

        def softmax(item, slot):
            p = item[0]
            sc = s_ref[slot, :, 0:widths[p]]
            m = jnp.max(sc, axis=-1, keepdims=True)
            p_ref[slot, :, 0:widths[p]] = jnp.exp2(sc - m).astype(BF16)
            m_ref[p, place(item)[3], :] = jnp.where(even, m[:Q_BLK], m[Q_BLK:])

        for t in range(depth):
            raw(items[t], t)
        for t in range(depth):
            softmax(items[t], t)
            raw(items[t + depth], t)
        for t, item in enumerate(items):
            values(item, t % depth)
            finish(t)
            if t + depth < len(items):
                softmax(items[t + depth], t % depth)
            if t + 2 * depth < len(items):
                raw(items[t + 2 * depth], t % depth)

    def emit(i, carry):
        rows = pl.ds(pl.multiple_of(i * Q_BLK, Q_BLK), Q_BLK)
        o_ref[0, rows, :] = nat_ref[rows, :].astype(BF16)
        return carry

    lax.fori_loop(0, n_blk, emit, 0)


def _att_call(consts, q, k, v, *, bounded):
    bsz, s, _ = q.shape
    blk = pl.BlockSpec((1, s, LANES), lambda h, b: (b, 0, h))
    seq3 = pltpu.VMEM((3, s, LANES), F32)
    depth = 12 if bounded else 3
    scratch = [pltpu.VMEM((2, 3, 2 * Q_BLK, K_WIN), F32),
               pltpu.VMEM((2 * Q_BLK, Q_BLK), F32),
               pltpu.VMEM((s, LANES), F32),
               pltpu.VMEM((s, LANES), F32),
               pltpu.VMEM((2, s, LANES), BF16),
               pltpu.VMEM((2, s, LANES), BF16),
               pltpu.VMEM((3, s, 2 * LANES), BF16),
               pltpu.VMEM((depth, 2 * Q_BLK, K_WIN), BF16),
               seq3, seq3]
    if not bounded:
        scratch += [pltpu.VMEM((depth, 2 * Q_BLK, K_WIN), F32), seq3]
    return pl.pallas_call(
        functools.partial(_att_kernel, bounded=bounded),
        grid=(D_ATT // LANES, bsz),
        in_specs=[pl.BlockSpec(memory_space=pltpu.SMEM), blk, blk, blk],
        out_specs=blk,
        out_shape=jax.ShapeDtypeStruct((bsz, s, D_ATT), BF16),
        scratch_shapes=scratch,
        compiler_params=pltpu.CompilerParams(dimension_semantics=("arbitrary", "arbitrary"),
                                             vmem_limit_bytes=48 * MIB),
        name="att_bounded" if bounded else "att_rowmax",
    )(consts, q, k, v)


def _att(q, k, v, g_q, g_k):
    s = q.shape[1]
    assert s // DILATIONS[2] == Q_BLK and DILATIONS[2] == DILATIONS[1] ** 2
    slopes = jnp.asarray(2.0 ** (-8.0 * np.arange(1, N_ATT_HEADS + 1) / N_ATT_HEADS), dtype=F32)
    bound = (1.02 * LOG2E * HEAD_DIM ** 0.5) * jnp.max(jnp.abs(g_q)) * jnp.max(jnp.abs(g_k))
    pad = jnp.zeros((N_ATT_HEADS - 1,), F32)

    def consts(shift):
        return jnp.concatenate([slopes, jnp.reshape(shift, (1,)).astype(F32), pad])

    return lax.cond(2.0 * bound < MAX_SOFTMAX_SPAN,
                    lambda: _att_call(consts(bound), q, k, v, bounded=True),
                    lambda: _att_call(consts(0.0), q, k, v, bounded=False))


_CONV_LANES = [slice(c * LANES, (c + 1) * LANES) for c in range(D_CONV // LANES)]


def _conv_stage(up_ref, u_ref, un_ref, upad_ref, has_prev, has_next):
    ts = u_ref.shape[1]
    for c, cs in enumerate(_CONV_LANES):
        upad_ref[c, 0:CONV_HALO, :] = jnp.where(has_prev, up_ref[0, :, cs].astype(F32), 0.0)
        upad_ref[c, CONV_HALO:CONV_HALO + ts, :] = u_ref[0, :, cs].astype(F32)
        upad_ref[c, CONV_HALO + ts:, :] = jnp.where(has_next, un_ref[0, :, cs].astype(F32), 0.0)


def _conv_rows(t0, zero, w_ref, b_ref, g_ref, beta_ref, upad_ref, put):
    conv = []
    for c, cs in enumerate(_CONV_LANES):
        acc = jnp.broadcast_to(b_ref[:, cs], (CONV_GROUP, LANES)) + jnp.tile(zero, (CONV_GROUP // 8, 1))
        for k in range(CONV_WIDTH):
            lo = t0 + CONV_HALO - CONV_PAD + k
            acc = acc + w_ref[k:k + 1, cs] * upad_ref[c, lo:lo + CONV_GROUP, :]
        conv.append(acc)
    tot = conv[0] + conv[1] + conv[2] + conv[3]
    mu = jnp.sum(tot, axis=-1, keepdims=True) * (1.0 / D_CONV)
    cen = [t - mu for t in conv]
    sq = cen[0] * cen[0] + cen[1] * cen[1] + cen[2] * cen[2] + cen[3] * cen[3]
    rstd = lax.rsqrt(jnp.sum(sq, axis=-1, keepdims=True) * (1.0 / D_CONV) + EPS)
    out = []
    for c, cs in enumerate(_CONV_LANES):
        z = (cen[c] * rstd) * g_ref[:, cs] + beta_ref[:, cs]
        out.append(z * jax.nn.sigmoid(z))
        put(t0, cs, out[-1].astype(BF16))
    return (out[0] + out[1] + out[2] + out[3])[0:8, :]


def _mix_ffn_kernel(x_ref, ya_ref, mod_ref, up_ref, u_ref, un_ref, cw_ref, cb_ref, cg_ref,
                    cbeta_ref, zero_ref, g_ref, wo_ref, wg_ref, wu_ref, wd_ref, o_ref,
                    yc_ref, upad_ref, x1_ref, h_ref, acc_ref, *, tiles_per_seq):
    n = pl.program_id(0)
    last = pl.num_programs(0) - 2

    @pl.when(n == 0)
    def _no_previous_tile():
        yc_ref[1] = jnp.zeros(yc_ref.shape[1:], BF16)

    gate_m = mod_ref[0, 2:3, :]
    shift = mod_ref[0, 3:4, :]
    scale = mod_ref[0, 4:5, :]
    gate_f = mod_ref[0, 5:6, :]
    mix = _dot(yc_ref[(n + 1) % 2], wo_ref[0:D_CONV, :]) + _dot(ya_ref[0], wo_ref[D_CONV:, :])
    x1 = x_ref[0] + gate_m * mix
    x1_ref[...] = x1
    r = lax.rsqrt(jnp.mean(x1 * x1, axis=-1, keepdims=True) + EPS)
    h_ref[...] = (((x1 * r) * g_ref[...]) * (1.0 + scale) + shift).astype(BF16)

    tile = jnp.minimum(n, last) % tiles_per_seq

    def put(t0, cs, rows):
        yc_ref[n % 2, t0:t0 + rows.shape[0], cs] = rows

    def exact_zero(v):
        return pltpu.bitcast(pltpu.bitcast(v, jnp.int32) & zero_ref[...], F32)

    _conv_stage(up_ref, u_ref, un_ref, upad_ref, tile > 0, tile < tiles_per_seq - 1)

    chunks = list(range(0, D_FF, FF_CHUNK))
    groups = list(range(0, x_ref.shape[1], CONV_GROUP))
    hosts = len(chunks) - 1
    share = [groups[(j * len(groups)) // hosts:((j + 1) * len(groups)) // hosts]
             for j in range(hosts)] + [[]]
    acc_ref[...] = jnp.zeros_like(acc_ref)
    for j, c0 in enumerate(chunks):
        h = h_ref[...]
        g = _dot(h, wg_ref[:, c0:c0 + FF_CHUNK])
        u = _dot(h, wu_ref[:, c0:c0 + FF_CHUNK])
        acc_ref[...] += _dot(((g * jax.nn.sigmoid(g)) * u).astype(BF16), wd_ref[c0:c0 + FF_CHUNK, :])
        if share[j]:
            start = exact_zero(acc_ref[8:16, 0:LANES])
            for t0 in share[j]:
                tok = _conv_rows(t0, start, cw_ref, cb_ref, cg_ref, cbeta_ref, upad_ref, put)
                acc_ref[0:8, 0:LANES] += exact_zero(tok)
    o_ref[0] = x1_ref[...] + gate_f * acc_ref[...]


def _mix_ffn(x, u, ya, mod, w_dw, b_dw, g_ln, b_ln, g_ffn, w_out, w_gate, w_up, w_down):
    bsz, s, d = x.shape
    ts = TOKEN_TILE
    nt = s // ts
    total = bsz * nt
    hb = ts // CONV_HALO

    def done(n):
        f = jnp.maximum(n - 1, 0)
        return f // nt, f % nt

    def conv(n):
        c = jnp.minimum(n, total - 1)
        return c // nt, c % nt

    const = dict(pipeline_mode=pl.Buffered(1))
    fix = lambda n: (0, 0)
    vec = pl.BlockSpec((1, D_CONV), fix, **const)
    halo = lambda edge: pl.BlockSpec((1, CONV_HALO, D_CONV), edge)
    return pl.pallas_call(
        functools.partial(_mix_ffn_kernel, tiles_per_seq=nt),
        grid=(total + 1,),
        in_specs=[pl.BlockSpec((1, ts, d), lambda n: (*done(n), 0)),
                  pl.BlockSpec((1, ts, D_ATT), lambda n: (*done(n), 0)),
                  pl.BlockSpec((1, N_MOD, d), lambda n: (done(n)[0], 0, 0)),
                  halo(lambda n: (conv(n)[0], jnp.maximum(conv(n)[1] * hb - 1, 0), 0)),
                  pl.BlockSpec((1, ts, D_CONV), lambda n: (*conv(n), 0)),
                  halo(lambda n: (conv(n)[0], jnp.minimum((conv(n)[1] + 1) * hb, nt * hb - 1), 0)),
                  pl.BlockSpec((CONV_WIDTH, D_CONV), fix, **const), vec, vec, vec,
                  pl.BlockSpec((8, LANES), fix, **const),
                  pl.BlockSpec((1, d), fix, **const),
                  pl.BlockSpec((d, d), fix, **const),
                  pl.BlockSpec((d, D_FF), fix, **const),
                  pl.BlockSpec((d, D_FF), fix, **const),
                  pl.BlockSpec((D_FF, d), fix, **const)],
        out_specs=pl.BlockSpec((1, ts, d), lambda n: (*done(n), 0)),
        out_shape=jax.ShapeDtypeStruct((bsz, s, d), F32),
        scratch_shapes=[pltpu.VMEM((2, ts, D_CONV), BF16),
                        pltpu.VMEM((D_CONV // LANES, ts + 2 * CONV_HALO, LANES), F32),
                        pltpu.VMEM((ts, d), F32), pltpu.VMEM((ts, d), BF16),
                        pltpu.VMEM((ts, d), F32)],
        compiler_params=pltpu.CompilerParams(dimension_semantics=("arbitrary",),
                                             vmem_limit_bytes=56 * MIB),
        name="mix_ffn",
    )(x, ya, mod, u, u, u, w_dw, b_dw.reshape(1, D_CONV), g_ln.reshape(1, D_CONV),
      b_ln.reshape(1, D_CONV), jnp.zeros((8, LANES), jnp.int32), g_ffn.reshape(1, d),
      w_out, w_gate, w_up, w_down)


def kernel(x, c, w_ada, b_ada, g_mix, w_in, w_dw, b_dw, g_conv_ln, b_conv_ln, g_q, g_k,
           w_out, g_ffn, w_gate, w_up, w_down):
    bsz, s, d = x.shape
    for l in range(w_ada.shape[0]):
        mod = _ada(c, w_ada[l], b_ada[l]).reshape(bsz, N_MOD, d)
        u, q, k, v = _inp(x, mod, g_mix[l], w_in[l].astype(BF16), g_q[l], g_k[l])
        ya = _att(q, k, v, g_q[l], g_k[l])
        x = _mix_ffn(x, u, ya, mod, w_dw[l], b_dw[l], g_conv_ln[l], b_conv_ln[l], g_ffn[l],
                     w_out[l].astype(BF16), w_gate[l].astype(BF16), w_up[l].astype(BF16),
                     w_down[l].astype(BF16))
    return x
```

```python
import functools

import jax
import jax.numpy as jnp
import numpy as np
from jax import lax
from jax.experimental import pallas as pl
from jax.experimental.pallas import tpu as pltpu

F32 = jnp.float32
BF16 = jnp.bfloat16

D_MODEL = 1024
D_CONV = 512
D_ATT = 512
N_ATT_HEADS = 8
HEAD_DIM = 64
CONV_WIDTH = 31
CONV_PAD = CONV_WIDTH // 2
DILATIONS = (1, 4, 16)
RADIUS = 64
D_IN = 2 * D_CONV + 3 * D_ATT
D_FF = 2816
N_MOD = 6
EPS = 1e-6
NEG_INF = -1e30
LOG2E = 1.4426950408889634

LANES = 128
MXU_DIM = 256
MIB = 1024 * 1024
TOKEN_TILE = 512
INP_TILES = 4
Q_BLK = 128
K_WIN = Q_BLK + 2 * RADIUS
FF_CHUNK = MXU_DIM
CONV_HALO = 16
CONV_GROUP = 32
MAX_SOFTMAX_SPAN = 100.0


def _dot(a, b):
    return jnp.dot(a, b, preferred_element_type=F32)


def _split_bf16(a):
    hi = a.astype(BF16)
    lo = (a - hi.astype(F32)).astype(BF16)
    return hi, lo


def _ada_kernel(c_ref, w_ref, b_ref, o_ref):
    c = c_ref[...]
    a_hi, a_lo = _split_bf16(c * jax.nn.sigmoid(c))
    w_hi, w_lo = _split_bf16(w_ref[...])
    o_ref[...] = _dot(a_hi, w_hi) + _dot(a_hi, w_lo) + _dot(a_lo, w_hi) + b_ref[...]


def _ada(c, w, b):
    bsz, d = c.shape
    n = w.shape[1]
    tn = 1024
    return pl.pallas_call(
        _ada_kernel,
        grid=(n // tn,),
        in_specs=[pl.BlockSpec((bsz, d), lambda j: (0, 0)),
                  pl.BlockSpec((d, tn), lambda j: (0, j)),
                  pl.BlockSpec((1, tn), lambda j: (0, j))],
        out_specs=pl.BlockSpec((bsz, tn), lambda j: (0, j)),
        out_shape=jax.ShapeDtypeStruct((bsz, n), F32),
        compiler_params=pltpu.CompilerParams(dimension_semantics=("arbitrary",),
                                             vmem_limit_bytes=32 * MIB),
        name="ada",
    )(c, w, b.reshape(1, n))


def _head_norm(t, ones, gain):
    sq = (t * t).astype(BF16)
    ssq = jnp.concatenate([_dot(sq[:, c:c + MXU_DIM], ones) for c in range(0, D_ATT, MXU_DIM)],
                          axis=1)
    return t * lax.rsqrt(ssq * (1.0 / HEAD_DIM) + EPS) * gain


def _inp_kernel(x_ref, mod_ref, g_ref, w_ref, ones_ref, gq_ref, gk_ref,
                u_ref, q_ref, k_ref, v_ref):
    shift = mod_ref[0, 0:1, :]
    scale = mod_ref[0, 1:2, :]
    ones = ones_ref[...]
    o = 2 * D_CONV
    for t0 in range(0, x_ref.shape[1], TOKEN_TILE):
        rows = slice(t0, t0 + TOKEN_TILE)
        x = x_ref[0, rows, :]
        r = lax.rsqrt(jnp.mean(x * x, axis=-1, keepdims=True) + EPS)
        h = ((x * r) * g_ref[...]) * (1.0 + scale) + shift
        hb = h.astype(BF16)
        a = _dot(hb, w_ref[:, 0:D_CONV])
        g = _dot(hb, w_ref[:, D_CONV:2 * D_CONV])
        u_ref[0, rows, :] = (a * jax.nn.sigmoid(g)).astype(BF16)
        q = _dot(hb, w_ref[:, o:o + D_ATT])
        q_ref[0, rows, :] = (_head_norm(q, ones, gq_ref[...])
                             * (HEAD_DIM ** -0.5 * LOG2E)).astype(BF16)
        k = _dot(hb, w_ref[:, o + D_ATT:o + 2 * D_ATT])
        k_ref[0, rows, :] = _head_norm(k, ones, gk_ref[...]).astype(BF16)
        v_ref[0, rows, :] = _dot(hb, w_ref[:, o + 2 * D_ATT:o + 3 * D_ATT]).astype(BF16)


def _inp(x, mod, g_mix, w_in, g_q, g_k):
    bsz, s, d = x.shape
    ts = INP_TILES * TOKEN_TILE
    head = np.arange(MXU_DIM) // HEAD_DIM
    ones = jnp.asarray(head[:, None] == head[None, :], dtype=BF16)
    gq = jnp.tile(g_q, N_ATT_HEADS).reshape(1, D_ATT)
    gk = jnp.tile(g_k, N_ATT_HEADS).reshape(1, D_ATT)
    const = dict(pipeline_mode=pl.Buffered(1))
    row = lambda b, i: (b, i, 0)
    fix = lambda b, i: (0, 0)
    out = jax.ShapeDtypeStruct((bsz, s, D_ATT), BF16)
    return pl.pallas_call(
        _inp_kernel,
        grid=(bsz, s // ts),
        in_specs=[pl.BlockSpec((1, ts, d), row),
                  pl.BlockSpec((1, N_MOD, d), lambda b, i: (b, 0, 0)),
                  pl.BlockSpec((1, d), fix, **const),
                  pl.BlockSpec((d, D_IN), fix, **const),
                  pl.BlockSpec((MXU_DIM, MXU_DIM), fix, **const),
                  pl.BlockSpec((1, D_ATT), fix, **const),
                  pl.BlockSpec((1, D_ATT), fix, **const)],
        out_specs=[pl.BlockSpec((1, ts, D_ATT), row)] * 4,
        out_shape=[out] * 4,
        compiler_params=pltpu.CompilerParams(dimension_semantics=("arbitrary", "arbitrary"),
                                             vmem_limit_bytes=56 * MIB),
        name="inp",
    )(x, mod, g_mix.reshape(1, d), w_in, ones, gq, gk)


def _att_kernel(consts_ref, q_ref, k_ref, v_ref, o_ref,
                bias_ref, bias3_ref, nat_ref, p4_ref, qp_ref, kp_ref, va_ref,
                p_ref, acc_ref, l_ref, *row_max_scratch, bounded):
    hp = pl.program_id(0)
    s = q_ref.shape[1]
    n_blk = s // Q_BLK
    d2, d3 = DILATIONS[1], DILATIONS[2]
    lane = lax.broadcasted_iota(jnp.int32, (1, LANES), 1)
    even = lane < HEAD_DIM

    @pl.when(pl.program_id(1) == 0)
    def _init_tables():
        def table(width, off, dil):
            row = lax.broadcasted_iota(jnp.int32, (2 * Q_BLK, width), 0)
            col = lax.broadcasted_iota(jnp.int32, (2 * Q_BLK, width), 1)
            slope = jnp.where(row < Q_BLK, consts_ref[2 * hp], consts_ref[2 * hp + 1])
            dist = jnp.abs(col - (row & (Q_BLK - 1)) - off)
            alibi = (-LOG2E * slope) * (dil * dist).astype(F32) - consts_ref[N_ATT_HEADS]
            return jnp.where(dist <= RADIUS, alibi, NEG_INF)

        for var, off in enumerate((0, RADIUS, 2 * RADIUS)):
            for p, dil in enumerate(DILATIONS[:2]):
                bias_ref[p, var] = table(K_WIN, off, dil)
        bias3_ref[...] = table(Q_BLK, 0, d3)
        for p in range(3):
            va_ref[p, :, LANES:2 * LANES] = jnp.ones((s, LANES), BF16)

    n4 = s // d2

    def stage(src_ref, put4, put16):
        for t0 in range(0, s, Q_BLK):
            nat_ref[t0:t0 + Q_BLK, :] = src_ref[0, t0:t0 + Q_BLK, :].astype(F32)
        for t0 in range(0, s, Q_BLK):
            r, u0 = t0 // n4, t0 % n4
            rows = nat_ref[pl.ds(r + d2 * u0, Q_BLK, stride=d2), :]
            p4_ref[t0:t0 + Q_BLK, :] = rows
            put4(t0, rows.astype(BF16))
        for r16 in range(d3):
            rows = p4_ref[pl.ds((r16 % d2) * n4 + r16 // d2, Q_BLK, stride=d3 // d2), :]
            put16(r16 * Q_BLK, rows.astype(BF16))

    def put_ref(ref, p):
        def put(t0, val):
            ref[p, t0:t0 + val.shape[0], 0:LANES] = val
        return put

    stage(q_ref, put_ref(qp_ref, 0), put_ref(qp_ref, 1))
    stage(k_ref, put_ref(kp_ref, 0), put_ref(kp_ref, 1))
    va_ref[0, :, 0:LANES] = v_ref[0]
    stage(v_ref, put_ref(va_ref, 1), put_ref(va_ref, 2))

    sub2 = n_blk // d2
    widths = (K_WIN, K_WIN, Q_BLK)

    def place(item):
        p, i = item
        rows = pl.ds(i * Q_BLK, Q_BLK)

        def window(j, n_sub):
            return (min(max(j * Q_BLK - RADIUS, 0), (n_sub - 2) * Q_BLK),
                    0 if j == 0 else (2 if j == n_sub - 1 else 1))

        if p == 0:
            ws, var = window(i, n_blk)
            return rows, pl.ds(ws, K_WIN), bias_ref.at[0, var], rows
        if p == 1:
            r, j = i // sub2, i % sub2
            ws, var = window(j, sub2)
            return rows, pl.ds(r * (sub2 * Q_BLK) + ws, K_WIN), bias_ref.at[1, var], rows
        return rows, rows, bias3_ref, pl.ds((i % d2) * n4 + i // d2, Q_BLK, stride=d3 // d2)

    q_refs = (q_ref.at[0], qp_ref.at[0], qp_ref.at[1])
    k_refs = (k_ref.at[0], kp_ref.at[0], kp_ref.at[1])

    def scores(item):
        rows, win, bias, _ = place(item)
        qb = q_refs[item[0]][rows, :]
        zero = jnp.zeros_like(qb)
        lhs = jnp.concatenate([jnp.where(even, qb, zero), jnp.where(even, zero, qb)], axis=0)
        sc = lax.dot_general(lhs, k_refs[item[0]][win, :], (((1,), (1,)), ((), ())),
                             preferred_element_type=F32)
        return sc + bias[...]

    def values(item, slot):
        p = item[0]
        _, win, _, dst = place(item)
        pv = _dot(p_ref[slot, :, 0:widths[p]], va_ref[p, win, :])
        acc_ref[p, dst, :] = jnp.where(even, pv[:Q_BLK, :LANES], pv[Q_BLK:, :LANES])
        l_ref[p, dst, :] = jnp.where(even, pv[:Q_BLK, LANES:], pv[Q_BLK:, LANES:])

    items = [(0, i) for i in range(n_blk)]
    for i in range(n_blk):
        items += [(1, i), (2, d2 * (i % d2) + i // d2)]
    depth = p_ref.shape[0]
    if not bounded:
        s_ref, m_ref = row_max_scratch

    def combine(c):
        rows = pl.ds(c * Q_BLK, Q_BLK)
        nat = pl.ds(c // sub2 + d2 * Q_BLK * (c % sub2), Q_BLK, stride=d2)
        if bounded:
            num = acc_ref[0, nat, :] + acc_ref[1, rows, :] + acc_ref[2, rows, :]
            den = l_ref[0, nat, :] + l_ref[1, rows, :] + l_ref[2, rows, :]
        else:
            m1, m2, m3 = m_ref[0, nat, :], m_ref[1, rows, :], m_ref[2, rows, :]
            mx = jnp.maximum(jnp.maximum(m1, m2), m3)
            w1, w2, w3 = jnp.exp2(m1 - mx), jnp.exp2(m2 - mx), jnp.exp2(m3 - mx)
            num = w1 * acc_ref[0, nat, :] + w2 * acc_ref[1, rows, :] + w3 * acc_ref[2, rows, :]
            den = w1 * l_ref[0, nat, :] + w2 * l_ref[1, rows, :] + w3 * l_ref[2, rows, :]
        nat_ref[nat, :] = num / den

    def finish(t):
        done = t + 1 - n_blk
        if done > 0 and done % (2 * sub2) == 0:
            r = done // (2 * sub2) - 1
            for c in range(r * sub2, (r + 1) * sub2):
                combine(c)

    if bounded:
        def probs(item, slot):
            p_ref[slot, :, 0:widths[item[0]]] = jnp.exp2(scores(item)).astype(BF16)

        for t in range(depth):
            probs(items[t], t)
        for t, item in enumerate(items):
            values(item, t % depth)
            finish(t)
            if t + depth < len(items):
                probs(items[t + depth], t % depth)
    else:
        def raw(item, slot):
            s_ref[slot, :, 0:widths[item[0]]] = scores(item)

        def softmax(item, slot):
            p = item[0]
            sc = s_ref[slot, :, 0:widths[p]]
            m = jnp.max(sc, axis=-1, keepdims=True)
            p_ref[slot, :, 0:widths[p]] = jnp.exp2(sc - m).astype(BF16)
            m_ref[p, place(item)[3], :] = jnp.where(even, m[:Q_BLK], m[Q_BLK:])

        for t in range(depth):
            raw(items[t], t)
        for t in range(depth):
            softmax(items[t], t)
            raw(items[t + depth], t)
        for t, item in enumerate(items):
            values(item, t % depth)
            finish(t)
            if t + depth < len(items):
                softmax(items[t + depth], t % depth)
            if t + 2 * depth < len(items):
                raw(items[t + 2 * depth], t % depth)

    def emit(i, carry):
        rows = pl.ds(pl.multiple_of(i * Q_BLK, Q_BLK), Q_BLK)
        o_ref[0, rows, :] = nat_ref[rows, :].astype(BF16)
        return carry

    lax.fori_loop(0, n_blk, emit, 0)


def _att_call(consts, q, k, v, *, bounded):
    bsz, s, _ = q.shape
    blk = pl.BlockSpec((1, s, LANES), lambda h, b: (b, 0, h))
    seq3 = pltpu.VMEM((3, s, LANES), F32)
    depth = 12 if bounded else 3
    scratch = [pltpu.VMEM((2, 3, 2 * Q_BLK, K_WIN), F32),
               pltpu.VMEM((2 * Q_BLK, Q_BLK), F32),
               pltpu.VMEM((s, LANES), F32),
               pltpu.VMEM((s, LANES), F32),
               pltpu.VMEM((2, s, LANES), BF16),
               pltpu.VMEM((2, s, LANES), BF16),
               pltpu.VMEM((3, s, 2 * LANES), BF16),
               pltpu.VMEM((depth, 2 * Q_BLK, K_WIN), BF16),
               seq3, seq3]
    if not bounded:
        scratch += [pltpu.VMEM((depth, 2 * Q_BLK, K_WIN), F32), seq3]
    return pl.pallas_call(
        functools.partial(_att_kernel, bounded=bounded),
        grid=(D_ATT // LANES, bsz),
        in_specs=[pl.BlockSpec(memory_space=pltpu.SMEM), blk, blk, blk],
        out_specs=blk,
        out_shape=jax.ShapeDtypeStruct((bsz, s, D_ATT), BF16),
        scratch_shapes=scratch,
        compiler_params=pltpu.CompilerParams(dimension_semantics=("arbitrary", "arbitrary"),
                                             vmem_limit_bytes=48 * MIB),
        name="att_bounded" if bounded else "att_rowmax",
    )(consts, q, k, v)


def _att(q, k, v, g_q, g_k):
    s = q.shape[1]
    assert s // DILATIONS[2] == Q_BLK and DILATIONS[2] == DILATIONS[1] ** 2
    slopes = jnp.asarray(2.0 ** (-8.0 * np.arange(1, N_ATT_HEADS + 1) / N_ATT_HEADS), dtype=F32)
    bound = (1.02 * LOG2E * HEAD_DIM ** 0.5) * jnp.max(jnp.abs(g_q)) * jnp.max(jnp.abs(g_k))
    pad = jnp.zeros((N_ATT_HEADS - 1,), F32)

    def consts(shift):
        return jnp.concatenate([slopes, jnp.reshape(shift, (1,)).astype(F32), pad])

    return lax.cond(2.0 * bound < MAX_SOFTMAX_SPAN,
                    lambda: _att_call(consts(bound), q, k, v, bounded=True),
                    lambda: _att_call(consts(0.0), q, k, v, bounded=False))


_CONV_LANES = [slice(c * LANES, (c + 1) * LANES) for c in range(D_CONV // LANES)]


def _conv_stage(up_ref, u_ref, un_ref, upad_ref, has_prev, has_next):
    ts = u_ref.shape[1]
    for c, cs in enumerate(_CONV_LANES):
        upad_ref[c, 0:CONV_HALO, :] = jnp.where(has_prev, up_ref[0, :, cs].astype(F32), 0.0)
        upad_ref[c, CONV_HALO:CONV_HALO + ts, :] = u_ref[0, :, cs].astype(F32)
        upad_ref[c, CONV_HALO + ts:, :] = jnp.where(has_next, un_ref[0, :, cs].astype(F32), 0.0)


def _conv_rows(t0, zero, w_ref, b_ref, g_ref, beta_ref, upad_ref, put):
    conv = []
    for c, cs in enumerate(_CONV_LANES):
        acc = jnp.broadcast_to(b_ref[:, cs], (CONV_GROUP, LANES)) + jnp.tile(zero, (CONV_GROUP // 8, 1))
        for k in range(CONV_WIDTH):
            lo = t0 + CONV_HALO - CONV_PAD + k
            acc = acc + w_ref[k:k + 1, cs] * upad_ref[c, lo:lo + CONV_GROUP, :]
        conv.append(acc)
    tot = conv[0] + conv[1] + conv[2] + conv[3]
    mu = jnp.sum(tot, axis=-1, keepdims=True) * (1.0 / D_CONV)
    cen = [t - mu for t in conv]
    sq = cen[0] * cen[0] + cen[1] * cen[1] + cen[2] * cen[2] + cen[3] * cen[3]
    rstd = lax.rsqrt(jnp.sum(sq, axis=-1, keepdims=True) * (1.0 / D_CONV) + EPS)
    out = []
    for c, cs in enumerate(_CONV_LANES):
        z = (cen[c] * rstd) * g_ref[:, cs] + beta_ref[:, cs]
        out.append(z * jax.nn.sigmoid(z))
        put(t0, cs, out[-1].astype(BF16))
    return (out[0] + out[1] + out[2] + out[3])[0:8, :]


def _mix_ffn_kernel(x_ref, ya_ref, mod_ref, up_ref, u_ref, un_ref, cw_ref, cb_ref, cg_ref,
                    cbeta_ref, zero_ref, g_ref, wo_ref, wg_ref, wu_ref, wd_ref, o_ref,
                    yc_ref, upad_ref, x1_ref, h_ref, acc_ref, *, tiles_per_seq):
    n = pl.program_id(0)
    last = pl.num_programs(0) - 2

    @pl.when(n == 0)
    def _no_previous_tile():
        yc_ref[1] = jnp.zeros(yc_ref.shape[1:], BF16)

    gate_m = mod_ref[0, 2:3, :]
    shift = mod_ref[0, 3:4, :]
    scale = mod_ref[0, 4:5, :]
    gate_f = mod_ref[0, 5:6, :]
    mix = _dot(yc_ref[(n + 1) % 2], wo_ref[0:D_CONV, :]) + _dot(ya_ref[0], wo_ref[D_CONV:, :])
    x1 = x_ref[0] + gate_m * mix
    x1_ref[...] = x1
    r = lax.rsqrt(jnp.mean(x1 * x1, axis=-1, keepdims=True) + EPS)
    h_ref[...] = (((x1 * r) * g_ref[...]) * (1.0 + scale) + shift).astype(BF16)

    tile = jnp.minimum(n, last) % tiles_per_seq

    def put(t0, cs, rows):
        yc_ref[n % 2, t0:t0 + rows.shape[0], cs] = rows

    def exact_zero(v):
        return pltpu.bitcast(pltpu.bitcast(v, jnp.int32) & zero_ref[...], F32)

    _conv_stage(up_ref, u_ref, un_ref, upad_ref, tile > 0, tile < tiles_per_seq - 1)

    chunks = list(range(0, D_FF, FF_CHUNK))
    groups = list(range(0, x_ref.shape[1], CONV_GROUP))
    hosts = len(chunks) - 1
    share = [groups[(j * len(groups)) // hosts:((j + 1) * len(groups)) // hosts]
             for j in range(hosts)] + [[]]
    acc_ref[...] = jnp.zeros_like(acc_ref)
    for j, c0 in enumerate(chunks):
        h = h_ref[...]
        g = _dot(h, wg_ref[:, c0:c0 + FF_CHUNK])
        u = _dot(h, wu_ref[:, c0:c0 + FF_CHUNK])
        acc_ref[...] += _dot(((g * jax.nn.sigmoid(g)) * u).astype(BF16), wd_ref[c0:c0 + FF_CHUNK, :])
        if share[j]:
            start = exact_zero(acc_ref[8:16, 0:LANES])
            for t0 in share[j]:
                tok = _conv_rows(t0, start, cw_ref, cb_ref, cg_ref, cbeta_ref, upad_ref, put)
                acc_ref[0:8, 0:LANES] += exact_zero(tok)
    o_ref[0] = x1_ref[...] + gate_f * acc_ref[...]


def _mix_ffn(x, u, ya, mod, w_dw, b_dw, g_ln, b_ln, g_ffn, w_out, w_gate, w_up, w_down):
    bsz, s, d = x.shape
    ts = TOKEN_TILE
    nt = s // ts
    total = bsz * nt
    hb = ts // CONV_HALO

    def done(n):
        f = jnp.maximum(n - 1, 0)
        return f // nt, f % nt

    def conv(n):
        c = jnp.minimum(n, total - 1)
        return c // nt, c % nt

    const = dict(pipeline_mode=pl.Buffered(1))
    fix = lambda n: (0, 0)
    vec = pl.BlockSpec((1, D_CONV), fix, **const)
    halo = lambda edge: pl.BlockSpec((1, CONV_HALO, D_CONV), edge)
    return pl.pallas_call(
        functools.partial(_mix_ffn_kernel, tiles_per_seq=nt),
        grid=(total + 1,),
        in_specs=[pl.BlockSpec((1, ts, d), lambda n: (*done(n), 0)),
                  pl.BlockSpec((1, ts, D_ATT), lambda n: (*done(n), 0)),
                  pl.BlockSpec((1, N_MOD, d), lambda n: (done(n)[0], 0, 0)),
                  halo(lambda n: (conv(n)[0], jnp.maximum(conv(n)[1] * hb - 1, 0), 0)),
                  pl.BlockSpec((1, ts, D_CONV), lambda n: (*conv(n), 0)),
                  halo(lambda n: (conv(n)[0], jnp.minimum((conv(n)[1] + 1) * hb, nt * hb - 1), 0)),
                  pl.BlockSpec((CONV_WIDTH, D_CONV), fix, **const), vec, vec, vec,
                  pl.BlockSpec((8, LANES), fix, **const),
                  pl.BlockSpec((1, d), fix, **const),
                  pl.BlockSpec((d, d), fix, **const),
                  pl.BlockSpec((d, D_FF), fix, **const),
                  pl.BlockSpec((d, D_FF), fix, **const),
                  pl.BlockSpec((D_FF, d), fix, **const)],
        out_specs=pl.BlockSpec((1, ts, d), lambda n: (*done(n), 0)),
        out_shape=jax.ShapeDtypeStruct((bsz, s, d), F32),
        scratch_shapes=[pltpu.VMEM((2, ts, D_CONV), BF16),
                        pltpu.VMEM((D_CONV // LANES, ts + 2 * CONV_HALO, LANES), F32),
                        pltpu.VMEM((ts, d), F32), pltpu.VMEM((ts, d), BF16),
                        pltpu.VMEM((ts, d), F32)],
        compiler_params=pltpu.CompilerParams(dimension_semantics=("arbitrary",),
                                             vmem_limit_bytes=56 * MIB),
        name="mix_ffn",
    )(x, ya, mod, u, u, u, w_dw, b_dw.reshape(1, D_CONV), g_ln.reshape(1, D_CONV),
      b_ln.reshape(1, D_CONV), jnp.zeros((8, LANES), jnp.int32), g_ffn.reshape(1, d),
      w_out, w_gate, w_up, w_down)


def kernel(x, c, w_ada, b_ada, g_mix, w_in, w_dw, b_dw, g_conv_ln, b_conv_ln, g_q, g_k,
           w_out, g_ffn, w_gate, w_up, w_down):
    bsz, s, d = x.shape
    for l in range(w_ada.shape[0]):
        mod = _ada(c, w_ada[l], b_ada[l]).reshape(bsz, N_MOD, d)
        u, q, k, v = _inp(x, mod, g_mix[l], w_in[l].astype(BF16), g_q[l], g_k[l])
        ya = _att(q, k, v, g_q[l], g_k[l])
        x = _mix_ffn(x, u, ya, mod, w_dw[l], b_dw[l], g_conv_ln[l], b_conv_ln[l], g_ffn[l],
                     w_out[l].astype(BF16), w_gate[l].astype(BF16), w_up[l].astype(BF16),
                     w_down[l].astype(BF16))
    return x
```

```python
import functools

import jax
import jax.numpy as jnp
import numpy as np
from jax import lax
from jax.experimental import pallas as pl
from jax.experimental.pallas import tpu as pltpu

F32 = jnp.float32
BF16 = jnp.bfloat16

D_MODEL = 1024
D_CONV = 512
D_ATT = 512
N_ATT_HEADS = 8
HEAD_DIM = 64
CONV_WIDTH = 31
CONV_PAD = CONV_WIDTH // 2
DILATIONS = (1, 4, 16)
RADIUS = 64
D_IN = 2 * D_CONV + 3 * D_ATT
D_FF = 2816
N_MOD = 6
EPS = 1e-6
NEG_INF = -1e30
LOG2E = 1.4426950408889634

LANES = 128
MXU_DIM = 256
MIB = 1024 * 1024
TOKEN_TILE = 512
INP_TILES = 4
Q_BLK = 128
K_WIN = Q_BLK + 2 * RADIUS
FF_CHUNK = MXU_DIM
CONV_HALO = 16
CONV_GROUP = 32
MAX_SOFTMAX_SPAN = 100.0


def _dot(a, b):
    return jnp.dot(a, b, preferred_element_type=F32)


def _split_bf16(a):
    hi = a.astype(BF16)
    lo = (a - hi.astype(F32)).astype(BF16)
    return hi, lo


def _ada_kernel(c_ref, w_ref, b_ref, o_ref):
    c = c_ref[...]
    a_hi, a_lo = _split_bf16(c * jax.nn.sigmoid(c))
    w_hi, w_lo = _split_bf16(w_ref[...])
    o_ref[...] = _dot(a_hi, w_hi) + _dot(a_hi, w_lo) + _dot(a_lo, w_hi) + b_ref[...]


def _ada(c, w, b):
    bsz, d = c.shape
    n = w.shape[1]
    tn = 1024
    return pl.pallas_call(
        _ada_kernel,
        grid=(n // tn,),
        in_specs=[pl.BlockSpec((bsz, d), lambda j: (0, 0)),
                  pl.BlockSpec((d, tn), lambda j: (0, j)),
                  pl.BlockSpec((1, tn), lambda j: (0, j))],
        out_specs=pl.BlockSpec((bsz, tn), lambda j: (0, j)),
        out_shape=jax.ShapeDtypeStruct((bsz, n), F32),
        compiler_params=pltpu.CompilerParams(dimension_semantics=("arbitrary",),
                                             vmem_limit_bytes=32 * MIB),
        name="ada",
    )(c, w, b.reshape(1, n))


def _head_norm(t, ones, gain):
    sq = (t * t).astype(BF16)
    ssq = jnp.concatenate([_dot(sq[:, c:c + MXU_DIM], ones) for c in range(0, D_ATT, MXU_DIM)],
                          axis=1)
    return t * lax.rsqrt(ssq * (1.0 / HEAD_DIM) + EPS) * gain


def _inp_kernel(x_ref, mod_ref, g_ref, w_ref, ones_ref, gq_ref, gk_ref,
                u_ref, q_ref, k_ref, v_ref):
    shift = mod_ref[0, 0:1, :]
    scale = mod_ref[0, 1:2, :]
    ones = ones_ref[...]
    o = 2 * D_CONV
    for t0 in range(0, x_ref.shape[1], TOKEN_TILE):
        rows = slice(t0, t0 + TOKEN_TILE)
        x = x_ref[0, rows, :]
        r = lax.rsqrt(jnp.mean(x * x, axis=-1, keepdims=True) + EPS)
        h = ((x * r) * g_ref[...]) * (1.0 + scale) + shift
        hb = h.astype(BF16)
        a = _dot(hb, w_ref[:, 0:D_CONV])
        g = _dot(hb, w_ref[:, D_CONV:2 * D_CONV])
        u_ref[0, rows, :] = (a * jax.nn.sigmoid(g)).astype(BF16)
        q = _dot(hb, w_ref[:, o:o + D_ATT])
        q_ref[0, rows, :] = (_head_norm(q, ones, gq_ref[...])
                             * (HEAD_DIM ** -0.5 * LOG2E)).astype(BF16)
        k = _dot(hb, w_ref[:, o + D_ATT:o + 2 * D_ATT])
        k_ref[0, rows, :] = _head_norm(k, ones, gk_ref[...]).astype(BF16)
        v_ref[0, rows, :] = _dot(hb, w_ref[:, o + 2 * D_ATT:o + 3 * D_ATT]).astype(BF16)


def _inp(x, mod, g_mix, w_in, g_q, g_k):
    bsz, s, d = x.shape
    ts = INP_TILES * TOKEN_TILE
    head = np.arange(MXU_DIM) // HEAD_DIM
    ones = jnp.asarray(head[:, None] == head[None, :], dtype=BF16)
    gq = jnp.tile(g_q, N_ATT_HEADS).reshape(1, D_ATT)
    gk = jnp.tile(g_k, N_ATT_HEADS).reshape(1, D_ATT)
    const = dict(pipeline_mode=pl.Buffered(1))
    row = lambda b, i: (b, i, 0)
    fix = lambda b, i: (0, 0)
    out = jax.ShapeDtypeStruct((bsz, s, D_ATT), BF16)
    return pl.pallas_call(
        _inp_kernel,
        grid=(bsz, s // ts),
        in_specs=[pl.BlockSpec((1, ts, d), row),
                  pl.BlockSpec((1, N_MOD, d), lambda b, i: (b, 0, 0)),
                  pl.BlockSpec((1, d), fix, **const),
                  pl.BlockSpec((d, D_IN), fix, **const),
                  pl.BlockSpec((MXU_DIM, MXU_DIM), fix, **const),
                  pl.BlockSpec((1, D_ATT), fix, **const),
                  pl.BlockSpec((1, D_ATT), fix, **const)],
        out_specs=[pl.BlockSpec((1, ts, D_ATT), row)] * 4,
        out_shape=[out] * 4,
        compiler_params=pltpu.CompilerParams(dimension_semantics=("arbitrary", "arbitrary"),
                                             vmem_limit_bytes=56 * MIB),
        name="inp",
    )(x, mod, g_mix.reshape(1, d), w_in, ones, gq, gk)


def _att_kernel(consts_ref, q_ref, k_ref, v_ref, o_ref,
                bias_ref, bias3_ref, nat_ref, p4_ref, qp_ref, kp_ref, va_ref,
                p_ref, acc_ref, l_ref, *row_max_scratch, bounded):
    hp = pl.program_id(0)
    s = q_ref.shape[1]
    n_blk = s // Q_BLK
    d2, d3 = DILATIONS[1], DILATIONS[2]
    lane = lax.broadcasted_iota(jnp.int32, (1, LANES), 1)
    even = lane < HEAD_DIM

    @pl.when(pl.program_id(1) == 0)
    def _init_tables():
        def table(width, off, dil):
            row = lax.broadcasted_iota(jnp.int32, (2 * Q_BLK, width), 0)
            col = lax.broadcasted_iota(jnp.int32, (2 * Q_BLK, width), 1)
            slope = jnp.where(row < Q_BLK, consts_ref[2 * hp], consts_ref[2 * hp + 1])
            dist = jnp.abs(col - (row & (Q_BLK - 1)) - off)
            alibi = (-LOG2E * slope) * (dil * dist).astype(F32) - consts_ref[N_ATT_HEADS]
            return jnp.where(dist <= RADIUS, alibi, NEG_INF)

        for var, off in enumerate((0, RADIUS, 2 * RADIUS)):
            for p, dil in enumerate(DILATIONS[:2]):
                bias_ref[p, var] = table(K_WIN, off, dil)
        bias3_ref[...] = table(Q_BLK, 0, d3)
        for p in range(3):
            va_ref[p, :, LANES:2 * LANES] = jnp.ones((s, LANES), BF16)

    n4 = s // d2

    def stage(src_ref, put4, put16):
        for t0 in range(0, s, Q_BLK):
            nat_ref[t0:t0 + Q_BLK, :] = src_ref[0, t0:t0 + Q_BLK, :].astype(F32)
        for t0 in range(0, s, Q_BLK):
            r, u0 = t0 // n4, t0 % n4
            rows = nat_ref[pl.ds(r + d2 * u0, Q_BLK, stride=d2), :]
            p4_ref[t0:t0 + Q_BLK, :] = rows
            put4(t0, rows.astype(BF16))
        for r16 in range(d3):
            rows = p4_ref[pl.ds((r16 % d2) * n4 + r16 // d2, Q_BLK, stride=d3 // d2), :]
            put16(r16 * Q_BLK, rows.astype(BF16))

    def put_ref(ref, p):
        def put(t0, val):
            ref[p, t0:t0 + val.shape[0], 0:LANES] = val
        return put

    stage(q_ref, put_ref(qp_ref, 0), put_ref(qp_ref, 1))
    stage(k_ref, put_ref(kp_ref, 0), put_ref(kp_ref, 1))
    va_ref[0, :, 0:LANES] = v_ref[0]
    stage(v_ref, put_ref(va_ref, 1), put_ref(va_ref, 2))

    sub2 = n_blk // d2
    widths = (K_WIN, K_WIN, Q_BLK)

    def place(item):
        p, i = item
        rows = pl.ds(i * Q_BLK, Q_BLK)

        def window(j, n_sub):
            return (min(max(j * Q_BLK - RADIUS, 0), (n_sub - 2) * Q_BLK),
                    0 if j == 0 else (2 if j == n_sub - 1 else 1))

        if p == 0:
            ws, var = window(i, n_blk)
            return rows, pl.ds(ws, K_WIN), bias_ref.at[0, var], rows
        if p == 1:
            r, j = i // sub2, i % sub2
            ws, var = window(j, sub2)
            return rows, pl.ds(r * (sub2 * Q_BLK) + ws, K_WIN), bias_ref.at[1, var], rows
        return rows, rows, bias3_ref, pl.ds((i % d2) * n4 + i // d2, Q_BLK, stride=d3 // d2)

    q_refs = (q_ref.at[0], qp_ref.at[0], qp_ref.at[1])
    k_refs = (k_ref.at[0], kp_ref.at[0], kp_ref.at[1])

    def scores(item):
        rows, win, bias, _ = place(item)
        qb = q_refs[item[0]][rows, :]
        zero = jnp.zeros_like(qb)
        lhs = jnp.concatenate([jnp.where(even, qb, zero), jnp.where(even, zero, qb)], axis=0)
        sc = lax.dot_general(lhs, k_refs[item[0]][win, :], (((1,), (1,)), ((), ())),
                             preferred_element_type=F32)
        return sc + bias[...]

    def values(item, slot):
        p = item[0]
        _, win, _, dst = place(item)
        pv = _dot(p_ref[slot, :, 0:widths[p]], va_ref[p, win, :])
        acc_ref[p, dst, :] = jnp.where(even, pv[:Q_BLK, :LANES], pv[Q_BLK:, :LANES])
        l_ref[p, dst, :] = jnp.where(even, pv[:Q_BLK, LANES:], pv[Q_BLK:, LANES:])

    items = [(0, i) for i in range(n_blk)]
    for i in range(n_blk):
        items += [(1, i), (2, d2 * (i % d2) + i // d2)]
    depth = p_ref.shape[0]
    if not bounded:
        s_ref, m_ref = row_max_scratch

    def combine(c):
        rows = pl.ds(c * Q_BLK, Q_BLK)
        nat = pl.ds(c // sub2 + d2 * Q_BLK * (c % sub2), Q_BLK, stride=d2)
        if bounded:
            num = acc_ref[0, nat, :] + acc_ref[1, rows, :] + acc_ref[2, rows, :]
            den = l_ref[0, nat, :] + l_ref[1, rows, :] + l_ref[2, rows, :]
        else:
            m1, m2, m3 = m_ref[0, nat, :], m_ref[1, rows, :], m_ref[2, rows, :]
            mx = jnp.maximum(jnp.maximum(m1, m2), m3)
            w1, w2, w3 = jnp.exp2(m1 - mx), jnp.exp2(m2 - mx), jnp.exp2(m3 - mx)
            num = w1 * acc_ref[0, nat, :] + w2 * acc_ref[1, rows, :] + w3 * acc_ref[2, rows, :]
            den = w1 * l_ref[0, nat, :] + w2 * l_ref[1, rows, :] + w3 * l_ref[2, rows, :]
        nat_ref[nat, :] = num / den

    def finish(t):
        done = t + 1 - n_blk
        if done > 0 and done % (2 * sub2) == 0:
            r = done // (2 * sub2) - 1
            for c in range(r * sub2, (r + 1) * sub2):
                combine(c)

    if bounded:
        def probs(item, slot):
            p_ref[slot, :, 0:widths[item[0]]] = jnp.exp2(scores(item)).astype(BF16)

        for t in range(depth):
            probs(items[t], t)
        for t, item in enumerate(items):
            values(item, t % depth)
            finish(t)
            if t + depth < len(items):
                probs(items[t + depth], t % depth)
    else:
        def raw(item, slot):
            s_ref[slot, :, 0:widths[item[0]]] = scores(item)

        def softmax(item, slot):
            p = item[0]
            sc = s_ref[slot, :, 0:widths[p]]
            m = jnp.max(sc, axis=-1, keepdims=True)
            p_ref[slot, :, 0:widths[p]] = jnp.exp2(sc - m).astype(BF16)
            m_ref[p, place(item)[3], :] = jnp.where(even, m[:Q_BLK], m[Q_BLK:])

        for t in range(depth):
            raw(items[t], t)
        for t in range(depth):
            softmax(items[t], t)
            raw(items[t + depth], t)
        for t, item in enumerate(items):
            values(item, t % depth)
            finish(t)
            if t + depth < len(items):
                softmax(items[t + depth], t % depth)
            if t + 2 * depth < len(items):
                raw(items[t + 2 * depth], t % depth)

    def emit(i, carry):
        rows = pl.ds(pl.multiple_of(i * Q_BLK, Q_BLK), Q_BLK)
        o_ref[0, rows, :] = nat_ref[rows, :].astype(BF16)
        return carry

    lax.fori_loop(0, n_blk, emit, 0)


def _att_call(consts, q, k, v, *, bounded):
    bsz, s, _ = q.shape
    blk = pl.BlockSpec((1, s, LANES), lambda h, b: (b, 0, h))
    seq3 = pltpu.VMEM((3, s, LANES), F32)
    depth = 12 if bounded else 3
    scratch = [pltpu.VMEM((2, 3, 2 * Q_BLK, K_WIN), F32),
               pltpu.VMEM((2 * Q_BLK, Q_BLK), F32),
               pltpu.VMEM((s, LANES), F32),
               pltpu.VMEM((s, LANES), F32),
               pltpu.VMEM((2, s, LANES), BF16),
               pltpu.VMEM((2, s, LANES), BF16),
               pltpu.VMEM((3, s, 2 * LANES), BF16),
               pltpu.VMEM((depth, 2 * Q_BLK, K_WIN), BF16),
               seq3, seq3]
    if not bounded:
        scratch += [pltpu.VMEM((depth, 2 * Q_BLK, K_WIN), F32), seq3]
    return pl.pallas_call(
        functools.partial(_att_kernel, bounded=bounded),
        grid=(D_ATT // LANES, bsz),
        in_specs=[pl.BlockSpec(memory_space=pltpu.SMEM), blk, blk, blk],
        out_specs=blk,
        out_shape=jax.ShapeDtypeStruct((bsz, s, D_ATT), BF16),
        scratch_shapes=scratch,
        compiler_params=pltpu.CompilerParams(dimension_semantics=("arbitrary", "arbitrary"),
                                             vmem_limit_bytes=48 * MIB),
        name="att_bounded" if bounded else "att_rowmax",
    )(consts, q, k, v)


def _att(q, k, v, g_q, g_k):
    s = q.shape[1]
    assert s // DILATIONS[2] == Q_BLK and DILATIONS[2] == DILATIONS[1] ** 2
    slopes = jnp.asarray(2.0 ** (-8.0 * np.arange(1, N_ATT_HEADS + 1) / N_ATT_HEADS), dtype=F32)
    bound = (1.02 * LOG2E * HEAD_DIM ** 0.5) * jnp.max(jnp.abs(g_q)) * jnp.max(jnp.abs(g_k))
    pad = jnp.zeros((N_ATT_HEADS - 1,), F32)

    def consts(shift):
        return jnp.concatenate([slopes, jnp.reshape(shift, (1,)).astype(F32), pad])

    shift = jnp.minimum(bound, MAX_SOFTMAX_SPAN - bound)
    return _att_call(consts(shift), q, k, v, bounded=True)


_CONV_LANES = [slice(c * LANES, (c + 1) * LANES) for c in range(D_CONV // LANES)]


def _conv_stage(up_ref, u_ref, un_ref, upad_ref, has_prev, has_next):
    ts = u_ref.shape[1]
    for c, cs in enumerate(_CONV_LANES):
        upad_ref[c, 0:CONV_HALO, :] = jnp.where(has_prev, up_ref[0, :, cs].astype(F32), 0.0)
        upad_ref[c, CONV_HALO:CONV_HALO + ts, :] = u_ref[0, :, cs].astype(F32)
        upad_ref[c, CONV_HALO + ts:, :] = jnp.where(has_next, un_ref[0, :, cs].astype(F32), 0.0)


def _conv_rows(t0, zero, w_ref, b_ref, g_ref, beta_ref, upad_ref, put):
    conv = []
    for c, cs in enumerate(_CONV_LANES):
        acc = jnp.broadcast_to(b_ref[:, cs], (CONV_GROUP, LANES)) + jnp.tile(zero, (CONV_GROUP // 8, 1))
        for k in range(CONV_WIDTH):
            lo = t0 + CONV_HALO - CONV_PAD + k
            acc = acc + w_ref[k:k + 1, cs] * upad_ref[c, lo:lo + CONV_GROUP, :]
        conv.append(acc)
    tot = conv[0] + conv[1] + conv[2] + conv[3]
    mu = jnp.sum(tot, axis=-1, keepdims=True) * (1.0 / D_CONV)
    cen = [t - mu for t in conv]
    sq = cen[0] * cen[0] + cen[1] * cen[1] + cen[2] * cen[2] + cen[3] * cen[3]
    rstd = lax.rsqrt(jnp.sum(sq, axis=-1, keepdims=True) * (1.0 / D_CONV) + EPS)
    out = []
    for c, cs in enumerate(_CONV_LANES):
        z = (cen[c] * rstd) * g_ref[:, cs] + beta_ref[:, cs]
        out.append(z * jax.nn.sigmoid(z))
        put(t0, cs, out[-1].astype(BF16))
    return (out[0] + out[1] + out[2] + out[3])[0:8, :]


def _mix_ffn_kernel(x_ref, ya_ref, mod_ref, up_ref, u_ref, un_ref, cw_ref, cb_ref, cg_ref,
                    cbeta_ref, zero_ref, g_ref, wo_ref, wg_ref, wu_ref, wd_ref, o_ref,
                    yc_ref, upad_ref, x1_ref, h_ref, acc_ref, *, tiles_per_seq):
    n = pl.program_id(0)
    last = pl.num_programs(0) - 2

    @pl.when(n == 0)
    def _no_previous_tile():
        yc_ref[1] = jnp.zeros(yc_ref.shape[1:], BF16)

    gate_m = mod_ref[0, 2:3, :]
    shift = mod_ref[0, 3:4, :]
    scale = mod_ref[0, 4:5, :]
    gate_f = mod_ref[0, 5:6, :]
    mix = _dot(yc_ref[(n + 1) % 2], wo_ref[0:D_CONV, :]) + _dot(ya_ref[0], wo_ref[D_CONV:, :])
    x1 = x_ref[0] + gate_m * mix
    x1_ref[...] = x1
    r = lax.rsqrt(jnp.mean(x1 * x1, axis=-1, keepdims=True) + EPS)
    h_ref[...] = (((x1 * r) * g_ref[...]) * (1.0 + scale) + shift).astype(BF16)

    tile = jnp.minimum(n, last) % tiles_per_seq

    def put(t0, cs, rows):
        yc_ref[n % 2, t0:t0 + rows.shape[0], cs] = rows

    def exact_zero(v):
        return pltpu.bitcast(pltpu.bitcast(v, jnp.int32) & zero_ref[...], F32)

    _conv_stage(up_ref, u_ref, un_ref, upad_ref, tile > 0, tile < tiles_per_seq - 1)

    chunks = list(range(0, D_FF, FF_CHUNK))
    groups = list(range(0, x_ref.shape[1], CONV_GROUP))
    hosts = len(chunks) - 1
    share = [groups[(j * len(groups)) // hosts:((j + 1) * len(groups)) // hosts]
             for j in range(hosts)] + [[]]
    acc_ref[...] = jnp.zeros_like(acc_ref)
    for j, c0 in enumerate(chunks):
        h = h_ref[...]
        g = _dot(h, wg_ref[:, c0:c0 + FF_CHUNK])
        u = _dot(h, wu_ref[:, c0:c0 + FF_CHUNK])
        acc_ref[...] += _dot(((g * jax.nn.sigmoid(g)) * u).astype(BF16), wd_ref[c0:c0 + FF_CHUNK, :])
        if share[j]:
            start = exact_zero(acc_ref[8:16, 0:LANES])
            for t0 in share[j]:
                tok = _conv_rows(t0, start, cw_ref, cb_ref, cg_ref, cbeta_ref, upad_ref, put)
                acc_ref[0:8, 0:LANES] += exact_zero(tok)
    o_ref[0] = x1_ref[...] + gate_f * acc_ref[...]


def _mix_ffn(x, u, ya, mod, w_dw, b_dw, g_ln, b_ln, g_ffn, w_out, w_gate, w_up, w_down):
    bsz, s, d = x.shape
    ts = TOKEN_TILE
    nt = s // ts
    total = bsz * nt
    hb = ts // CONV_HALO

    def done(n):
        f = jnp.maximum(n - 1, 0)
        return f // nt, f % nt

    def conv(n):
        c = jnp.minimum(n, total - 1)
        return c // nt, c % nt

    const = dict(pipeline_mode=pl.Buffered(1))
    fix = lambda n: (0, 0)
    vec = pl.BlockSpec((1, D_CONV), fix, **const)
    halo = lambda edge: pl.BlockSpec((1, CONV_HALO, D_CONV), edge)
    return pl.pallas_call(
        functools.partial(_mix_ffn_kernel, tiles_per_seq=nt),
        grid=(total + 1,),
        in_specs=[pl.BlockSpec((1, ts, d), lambda n: (*done(n), 0)),
                  pl.BlockSpec((1, ts, D_ATT), lambda n: (*done(n), 0)),
                  pl.BlockSpec((1, N_MOD, d), lambda n: (done(n)[0], 0, 0)),
                  halo(lambda n: (conv(n)[0], jnp.maximum(conv(n)[1] * hb - 1, 0), 0)),
                  pl.BlockSpec((1, ts, D_CONV), lambda n: (*conv(n), 0)),
                  halo(lambda n: (conv(n)[0], jnp.minimum((conv(n)[1] + 1) * hb, nt * hb - 1), 0)),
                  pl.BlockSpec((CONV_WIDTH, D_CONV), fix, **const), vec, vec, vec,
                  pl.BlockSpec((8, LANES), fix, **const),
                  pl.BlockSpec((1, d), fix, **const),
                  pl.BlockSpec((d, d), fix, **const),
                  pl.BlockSpec((d, D_FF), fix, **const),
                  pl.BlockSpec((d, D_FF), fix, **const),
                  pl.BlockSpec((D_FF, d), fix, **const)],
        out_specs=pl.BlockSpec((1, ts, d), lambda n: (*done(n), 0)),
        out_shape=jax.ShapeDtypeStruct((bsz, s, d), F32),
        scratch_shapes=[pltpu.VMEM((2, ts, D_CONV), BF16),
                        pltpu.VMEM((D_CONV // LANES, ts + 2 * CONV_HALO, LANES), F32),
                        pltpu.VMEM((ts, d), F32), pltpu.VMEM((ts, d), BF16),
                        pltpu.VMEM((ts, d), F32)],
        compiler_params=pltpu.CompilerParams(dimension_semantics=("arbitrary",),
                                             vmem_limit_bytes=56 * MIB),
        name="mix_ffn",
    )(x, ya, mod, u, u, u, w_dw, b_dw.reshape(1, D_CONV), g_ln.reshape(1, D_CONV),
      b_ln.reshape(1, D_CONV), jnp.zeros((8, LANES), jnp.int32), g_ffn.reshape(1, d),
      w_out, w_gate, w_up, w_down)


def kernel(x, c, w_ada, b_ada, g_mix, w_in, w_dw, b_dw, g_conv_ln, b_conv_ln, g_q, g_k,
           w_out, g_ffn, w_gate, w_up, w_down):
    bsz, s, d = x.shape
    for l in range(w_ada.shape[0]):
        mod = _ada(c, w_ada[l], b_ada[l]).reshape(bsz, N_MOD, d)
        u, q, k, v = _inp(x, mod, g_mix[l], w_in[l].astype(BF16), g_q[l], g_k[l])
        ya = _att(q, k, v, g_q[l], g_k[l])
        x = _mix_ffn(x, u, ya, mod, w_dw[l], b_dw[l], g_conv_ln[l], b_conv_ln[l], g_ffn[l],
                     w_out[l].astype(BF16), w_gate[l].astype(BF16), w_up[l].astype(BF16),
                     w_down[l].astype(BF16))
    return x
```
